```python
import math
import jax, jax.numpy as jnp
from jax import lax
import numpy as np

D_MODEL = 1024
BATCH = 4
SEQ = 4096
DEPTH = 4
DEC_BATCH = 32
DEC_SEQ = 1
PAST_LEN = 8192
PAGE_SIZE = 128

N_MIXERS = 2
N_A_LAYERS = (DEPTH + 1) // 2
N_B_LAYERS = DEPTH // 2
A_DK = 64
A_DV = 2 * A_DK
A_HEADS = D_MODEL // A_DV
A_WIDTH = A_HEADS * A_DV
A_SCALE = A_DK ** -0.5
B_HEAD = 64
B_HEADS = D_MODEL // B_HEAD
DECAY_LORA = 64
AAA_LORA = 64
MV_LORA = 32
PLE_DIM = 256
Q_BLOCK = 128
NORM_EPS = 1e-6
SUBLN_EPS = 1e-5
LNX_EPS = 64e-5

kernel_name = "hybrid_diffattn_rwkv7_step"


def rms_norm(x, g=None, eps=NORM_EPS):
    xf = x.astype(jnp.float32)
    y = xf * lax.rsqrt(jnp.mean(xf * xf, axis=-1, keepdims=True) + eps)
    if g is not None:
        y = y * g.astype(jnp.float32)
    return y.astype(x.dtype)


def diff_attn_core(q, kvm, lam):
    scores = []
    for k, _, m in kvm:
        s = jnp.einsum('bqhcd,bkhcd->bhcqk', q, k).astype(jnp.float32) * A_SCALE
        if m is not None:
            s = jnp.where(m, s, -jnp.inf)
        scores.append(s)
    pr = jax.nn.softmax(jnp.concatenate(scores, axis=-1), axis=-1)
    attn = (pr[:, :, 0] - lam * pr[:, :, 1]).astype(q.dtype)
    out, off = 0, 0
    for k, v, _ in kvm:
        n = k.shape[1]
        out = out + jnp.einsum('bhqk,bkhe->bqhe', attn[..., off:off + n], v)
        off += n
    return out


def diff_attention(hn, j, layer_idx, W, past_k=None, past_v=None):
    B_, S_, _ = hn.shape
    q, k, v, z = jnp.split(hn @ W['w_in_a'][j], 4, axis=-1)
    q = q.reshape(B_, S_, A_HEADS, 2, A_DK)
    k = k.reshape(B_, S_, A_HEADS, 2, A_DK)
    v = v.reshape(B_, S_, A_HEADS, A_DV)
    lam_init = 0.8 - 0.6 * math.exp(-0.3 * layer_idx)
    lp = W['lam_a'][j].astype(jnp.float32)
    lam = jnp.exp(jnp.sum(lp[0] * lp[1])) - jnp.exp(jnp.sum(lp[2] * lp[3])) + lam_init
    if past_k is None:
        nb = S_ // Q_BLOCK
        qb = q.reshape(B_, nb, Q_BLOCK, A_HEADS, 2, A_DK).swapaxes(0, 1)
        kpos = jnp.arange(S_)

        def block(args):
            q_blk, bi = args
            qpos = bi * Q_BLOCK + jnp.arange(Q_BLOCK)
            return diff_attn_core(q_blk, [(k, v, kpos[None, :] <= qpos[:, None])], lam)

        o = lax.map(block, (qb, jnp.arange(nb)))
        o = o.swapaxes(0, 1).reshape(B_, S_, A_HEADS, A_DV)
    else:
        causal = jnp.tril(jnp.ones((S_, S_), dtype=bool))
        o = diff_attn_core(q, [(past_k, past_v, None), (k, v, causal)], lam)
    o = rms_norm(o, W['subln_a'][j], SUBLN_EPS) * (1.0 - lam_init)
    y = (o.reshape(B_, S_, A_WIDTH) * jax.nn.silu(z)) @ W['w_out_a'][j]
    return y, k, v


def _wkv_step(state, inp):
    r_t, w_t, k_t, v_t, a_t, b_t = inp
    sa = jnp.einsum('bhij,bhj->bhi', state, a_t)
    state = (state * w_t[:, :, None, :] + sa[..., None] * b_t[:, :, None, :]
             + v_t[..., None] * k_t[:, :, None, :])
    return state, jnp.einsum('bhij,bhj->bhi', state, r_t)


def rwkv7_time_mix(hn, j, shift_prev, wkv0, v_first, W):
    B_, S_, _ = hn.shape
    f32 = jnp.float32
    prev = jnp.concatenate([shift_prev[:, None, :].astype(hn.dtype), hn[:, :-1]], axis=1)
    mixed = hn[None] + (prev - hn)[None] * W['mix_b'][j][:, None, None, :]
    r, k, v, z = jnp.einsum('mbsd,mde->mbse', mixed[:4], W['w_rkvz_b'][j])
    xv, xw, xa = mixed[2], mixed[4], mixed[5]
    w = -jax.nn.softplus(-(W['w0_b'][j] + jnp.tanh(xw @ W['w1_b'][j]) @ W['w2_b'][j])) - 0.5
    a = jax.nn.sigmoid(W['a0_b'][j] + (xa @ W['a1_b'][j]) @ W['a2_b'][j])
    if v_first is None:
        v_first = v
    else:
        v = v + (v_first - v) * jax.nn.sigmoid(
            W['v0_b'][j - 1] + (xv @ W['v1_b'][j - 1]) @ W['v2_b'][j - 1])
    heads = lambda t: t.astype(f32).reshape(B_, S_, B_HEADS, B_HEAD)
    kk = heads(k * W['kk_b'][j])
    kk = kk / jnp.maximum(jnp.linalg.norm(kk, axis=-1, keepdims=True), 1e-12)
    k = k * (1 + (a - 1) * W['ka_b'][j])
    rh, kh, vh, ah = heads(r), heads(k), heads(v), heads(a)
    decay = jnp.exp(-jnp.exp(heads(w)))
    xs = tuple(t.swapaxes(0, 1) for t in (rh, decay, kh, vh, -kk, kk * ah))
    state_fin, o = lax.scan(_wkv_step, wkv0.astype(f32), xs)
    o = o.swapaxes(0, 1)
    mu = jnp.mean(o, axis=-1, keepdims=True)
    var = jnp.mean(jnp.square(o - mu), axis=-1, keepdims=True)
    o = ((o - mu) * lax.rsqrt(var + LNX_EPS)).reshape(B_, S_, D_MODEL)
    o = o * W['lnx_g_b'][j].astype(f32) + W['lnx_b_b'][j].astype(f32)
    bonus = jnp.sum(rh * kh * W['rk_b'][j].astype(f32), axis=-1, keepdims=True) * vh
    o = (o + bonus.reshape(B_, S_, D_MODEL)).astype(hn.dtype)
    y = (o * jax.nn.silu(z)) @ W['w_out_b'][j]
    return y, state_fin, hn[:, -1], v_first


def run_trunk(x, p, W, paged=None, wkv_init=None, shift_init=None):
    B_ = x.shape[0]
    h = x
    v_first = None
    new_k, new_v, new_wkv, new_shift = [], [], [], []
    for i in range(DEPTH):
        j = i // N_MIXERS
        hn = rms_norm(h, W['norm_g'][i])
        if i % N_MIXERS == 0:
            if paged is None:
                pk = pv = None
            else:
                ck, cv, pt = paged
                pk = ck[j, pt].reshape(pt.shape[0], -1, A_HEADS, 2, A_DK)
                pv = cv[j, pt].reshape(pt.shape[0], -1, A_HEADS, A_DV)
            y, k, v = diff_attention(hn, j, i, W, pk, pv)
            new_k.append(k)
            new_v.append(v)
        else:
            if wkv_init is None:
                s0 = jnp.zeros((B_, B_HEADS, B_HEAD, B_HEAD), jnp.float32)
                sh0 = jnp.zeros((B_, D_MODEL), h.dtype)
            else:
                s0, sh0 = wkv_init[j], shift_init[j]
            y, s_fin, sh, v_first = rwkv7_time_mix(hn, j, sh0, s0, v_first, W)
            new_wkv.append(s_fin)
            new_shift.append(sh)
        h = h + y
        gate = jax.nn.sigmoid(rms_norm(h) @ W['w_pg'][i])
        h = h + (p[i] @ W['w_pe'][i]) * gate
    out = rms_norm(h, W['norm_f'])
    return out, jnp.stack(new_k), jnp.stack(new_v), jnp.stack(new_wkv), jnp.stack(new_shift)


def setup_inputs(seed: int = 0) -> dict:
    key = jax.random.key(seed)
    ks = iter(jax.random.split(key, 48))
    nrm = lambda shape, s=1.0: jax.random.normal(next(ks), shape, jnp.float32) * s
    gain = lambda shape: 1.0 + nrm(shape, 0.02)
    n_pages = PAST_LEN // PAGE_SIZE
    n_used = DEC_BATCH * n_pages
    n_phys = n_used + max(1, n_used // 4)
    D = D_MODEL
    inp = {}
    inp['x_prompt'] = nrm((BATCH, SEQ, D))
    inp['x_sample'] = nrm((DEC_BATCH, DEC_SEQ, D))
    inp['cache_k'] = nrm((N_A_LAYERS, n_phys, PAGE_SIZE, A_HEADS, 2, A_DK))
    inp['cache_v'] = nrm((N_A_LAYERS, n_phys, PAGE_SIZE, A_HEADS, A_DV))
    inp['state_wkv'] = nrm((N_B_LAYERS, DEC_BATCH, B_HEADS, B_HEAD, B_HEAD), 0.5)
    inp['state_shift'] = nrm((N_B_LAYERS, DEC_BATCH, D))
    inp['page_table'] = jax.random.permutation(next(ks), n_phys)[:n_used].reshape(
        DEC_BATCH, n_pages).astype(jnp.int32)
    inp['p_prompt'] = nrm((DEPTH, BATCH, SEQ, PLE_DIM))
    inp['p_sample'] = nrm((DEPTH, DEC_BATCH, DEC_SEQ, PLE_DIM))
    inp['norm_g'] = gain((DEPTH, D))
    inp['norm_f'] = gain((D,))
    inp['w_pe'] = nrm((DEPTH, PLE_DIM, D), PLE_DIM ** -0.5)
    inp['w_pg'] = nrm((DEPTH, D, D), D ** -0.5)
    inp['w_in_a'] = nrm((N_A_LAYERS, D, 4 * A_WIDTH), D ** -0.5)
    inp['w_out_a'] = nrm((N_A_LAYERS, A_WIDTH, D), A_WIDTH ** -0.5)
    inp['lam_a'] = nrm((N_A_LAYERS, 4, A_DK), 0.1)
    inp['subln_a'] = gain((N_A_LAYERS, A_DV))
    inp['mix_b'] = jax.random.uniform(next(ks), (N_B_LAYERS, 6, D), jnp.float32)
    inp['w_rkvz_b'] = nrm((N_B_LAYERS, 4, D, D), D ** -0.5)
    inp['w0_b'] = jax.random.uniform(next(ks), (N_B_LAYERS, D), jnp.float32, -6.0, 1.0)
    inp['w1_b'] = nrm((N_B_LAYERS, D, DECAY_LORA), D ** -0.5)
    inp['w2_b'] = nrm((N_B_LAYERS, DECAY_LORA, D), 0.5 * DECAY_LORA ** -0.5)
    inp['a0_b'] = nrm((N_B_LAYERS, D), 0.1)
    inp['a1_b'] = nrm((N_B_LAYERS, D, AAA_LORA), D ** -0.5)
    inp['a2_b'] = nrm((N_B_LAYERS, AAA_LORA, D), 0.5 * AAA_LORA ** -0.5)
    inp['v0_b'] = nrm((N_B_LAYERS - 1, D), 0.1)
    inp['v1_b'] = nrm((N_B_LAYERS - 1, D, MV_LORA), D ** -0.5)
    inp['v2_b'] = nrm((N_B_LAYERS - 1, MV_LORA, D), 0.5 * MV_LORA ** -0.5)
    inp['kk_b'] = 0.85 + nrm((N_B_LAYERS, D), 0.02)
    inp['ka_b'] = gain((N_B_LAYERS, D))
    inp['rk_b'] = nrm((N_B_LAYERS, B_HEADS, B_HEAD), 0.1)
    inp['lnx_g_b'] = gain((N_B_LAYERS, D))
    inp['lnx_b_b'] = nrm((N_B_LAYERS, D), 0.01)
    inp['w_out_b'] = nrm((N_B_LAYERS, D, D), D ** -0.5)
    return inp


def reference(x_prompt, x_sample, cache_k, cache_v, state_wkv, state_shift, page_table,
              p_prompt, p_sample, norm_g, norm_f, w_pe, w_pg, w_in_a, w_out_a, lam_a, subln_a,
              mix_b, w_rkvz_b, w0_b, w1_b, w2_b, a0_b, a1_b, a2_b, v0_b, v1_b, v2_b,
              kk_b, ka_b, rk_b, lnx_g_b, lnx_b_b, w_out_b):
    W = dict(norm_g=norm_g, norm_f=norm_f, w_pe=w_pe, w_pg=w_pg, w_in_a=w_in_a,
             w_out_a=w_out_a, lam_a=lam_a, subln_a=subln_a, mix_b=mix_b, w_rkvz_b=w_rkvz_b,
             w0_b=w0_b, w1_b=w1_b, w2_b=w2_b, a0_b=a0_b, a1_b=a1_b, a2_b=a2_b,
             v0_b=v0_b, v1_b=v1_b, v2_b=v2_b, kk_b=kk_b, ka_b=ka_b, rk_b=rk_b,
             lnx_g_b=lnx_g_b, lnx_b_b=lnx_b_b, w_out_b=w_out_b)
    y_prompt, k_prompt, v_prompt, wkv_prompt, shift_prompt = run_trunk(x_prompt, p_prompt, W)
    y_sample, k_sample, v_sample, wkv_sample, shift_sample = run_trunk(
        x_sample, p_sample, W, paged=(cache_k, cache_v, page_table),
        wkv_init=state_wkv, shift_init=state_shift)
    return (y_prompt, y_sample, k_prompt, v_prompt, wkv_prompt, shift_prompt,
            k_sample, v_sample, wkv_sample, shift_sample)
```

```python
import functools
import math

import jax
import jax.numpy as jnp
from jax import lax
from jax.experimental import pallas as pl
from jax.experimental.pallas import tpu as pltpu

F32 = jnp.float32
BF16 = jnp.bfloat16

NORM_EPS = 1e-6
SUBLN_EPS = 1e-5
LNX_EPS = 64e-5
N_MIXERS = 2

A_DK = 64
A_DV = 2 * A_DK
B_HEAD = 64
LANES = 128
SCAN_HEADS = LANES // 2
SCAN_ROWS = B_HEAD // 2
VMEM_LIMIT = 48 * 1024 * 1024


def _cparams(*sem):
    return pltpu.CompilerParams(dimension_semantics=sem, vmem_limit_bytes=VMEM_LIMIT)


def _row_tile(t, cap):
    tile = min(t, cap)
    assert t % tile == 0, (t, tile)
    return tile


def _rms(x, eps):
    return x * lax.rsqrt(jnp.mean(x * x, axis=-1, keepdims=True) + eps)


def _sigmoid(x):
    return 1.0 / (1.0 + jnp.exp(-x))


def _dot(a, b):
    return jnp.dot(a, b, preferred_element_type=F32)


def _dot_nt(a, b):
    return lax.dot_general(a, b, (((1,), (1,)), ((), ())), preferred_element_type=F32)


def _norm_kernel(x_ref, g_ref, o_ref):
    o_ref[...] = _rms(x_ref[...], NORM_EPS) * g_ref[...]


def rms_norm_call(x, g):
    t, d = x.shape
    tm = _row_tile(t, 512)
    return pl.pallas_call(
        _norm_kernel,
        grid=(t // tm,),
        in_specs=[pl.BlockSpec((tm, d), lambda i: (i, 0)), pl.BlockSpec((1, d), lambda i: (0, 0))],
        out_specs=pl.BlockSpec((tm, d), lambda i: (i, 0)),
        out_shape=jax.ShapeDtypeStruct((t, d), F32),
        compiler_params=_cparams("parallel"),
        name="rms_norm",
    )(x, g.reshape(1, d))


def _norm_mm_kernel(x_ref, g_ref, w_ref, o_ref, xn_ref):
    @pl.when(pl.program_id(1) == 0)
    def _():
        xn_ref[...] = (_rms(x_ref[...], NORM_EPS) * g_ref[...]).astype(BF16)

    o_ref[...] = _dot(xn_ref[...], w_ref[...])


def norm_matmul(x, g, w):
    t, d = x.shape
    n = w.shape[1]
    tm = _row_tile(t, 512)
    tn = _row_tile(n, 1024)
    return pl.pallas_call(
        _norm_mm_kernel,
        grid=(t // tm, n // tn),
        in_specs=[pl.BlockSpec((tm, d), lambda i, j: (i, 0)),
                  pl.BlockSpec((1, d), lambda i, j: (0, 0)),
                  pl.BlockSpec((d, tn), lambda i, j: (0, j))],
        out_specs=pl.BlockSpec((tm, tn), lambda i, j: (i, j)),
        out_shape=jax.ShapeDtypeStruct((t, n), F32),
        scratch_shapes=[pltpu.VMEM((tm, d), BF16)],
        compiler_params=_cparams("parallel", "arbitrary"),
        name="norm_matmul",
    )(x, g.reshape(1, d), w)


def _lam(lam_ref, lam_init):
    lp = lam_ref[...]
    s1 = jnp.sum(lp[0:1] * lp[1:2], axis=-1, keepdims=True)
    s2 = jnp.sum(lp[2:3] * lp[3:4], axis=-1, keepdims=True)
    return jnp.exp(s1) - jnp.exp(s2) + lam_init


def _flash_kernel(lam_ref, g_ref, q_ref, k_ref, v_ref, o_ref, kb_ref, vb_ref, m_ref, l_ref, acc_ref,
                  *, tq, lam_init):
    qi = pl.program_id(2)

    @pl.when(qi == 0)
    def _():
        kb_ref[...] = k_ref[0].astype(BF16)
        vb_ref[...] = v_ref[0].astype(BF16)

    q = q_ref[0] * (A_DK ** -0.5)
    lane = lax.broadcasted_iota(jnp.int32, q.shape, 1)
    qs = [jnp.where((lane >= A_DK) == (c == 1), q, 0.0).astype(BF16) for c in range(2)]
    m_ref[...] = jnp.full(m_ref.shape, -jnp.inf, F32)
    l_ref[...] = jnp.zeros(l_ref.shape, F32)
    acc_ref[...] = jnp.zeros(acc_ref.shape, F32)

    def block(kb, masked):
        start = pl.multiple_of(kb * tq, tq)
        k = kb_ref[pl.ds(start, tq), :]
        v = vb_ref[pl.ds(start, tq), :]
        for c in range(2):
            s = _dot_nt(qs[c], k)
            if masked:
                row = lax.broadcasted_iota(jnp.int32, s.shape, 0)
                col = lax.broadcasted_iota(jnp.int32, s.shape, 1)
                s = jnp.where(col <= row, s, -jnp.inf)
            m_prev = m_ref[c]
            m_new = jnp.maximum(m_prev, jnp.max(s, axis=-1, keepdims=True))
            alpha = jnp.exp(m_prev - m_new)
            p = jnp.exp(s - m_new)
            l_ref[c] = alpha * l_ref[c] + jnp.sum(p, axis=-1, keepdims=True)
            acc_ref[c] = alpha * acc_ref[c] + _dot(p.astype(BF16), v)
            m_ref[c] = m_new

    def body(kb, carry):
        block(kb, False)
        return carry

    lax.fori_loop(0, qi, body, 0)
    block(qi, True)

    lam = _lam(lam_ref, lam_init)
    o = acc_ref[0] / l_ref[0] - lam * (acc_ref[1] / l_ref[1])
    o_ref[0] = _rms(o, SUBLN_EPS) * g_ref[...] * (1.0 - lam_init)


def flash_diff_attention(qkvz, lam_p, subln, lam_init, n_heads, tq_cap=256):
    b, s, _ = qkvz.shape
    tq = _row_tile(s, tq_cap)
    kern = functools.partial(_flash_kernel, tq=tq, lam_init=lam_init)
    return pl.pallas_call(
        kern,
        grid=(b, n_heads, s // tq),
        in_specs=[pl.BlockSpec((4, A_DK), lambda bi, h, qi: (0, 0)),
                  pl.BlockSpec((1, A_DV), lambda bi, h, qi: (0, 0)),
                  pl.BlockSpec((1, tq, A_DV), lambda bi, h, qi: (bi, qi, h)),
                  pl.BlockSpec((1, s, A_DV), lambda bi, h, qi: (bi, 0, n_heads + h)),
                  pl.BlockSpec((1, s, A_DV), lambda bi, h, qi: (bi, 0, 2 * n_heads + h))],
        out_specs=pl.BlockSpec((1, tq, A_DV), lambda bi, h, qi: (bi, qi, h)),
        out_shape=jax.ShapeDtypeStruct((b, s, n_heads * A_DV), F32),
        scratch_shapes=[pltpu.VMEM((s, A_DV), BF16), pltpu.VMEM((s, A_DV), BF16),
                        pltpu.VMEM((2, tq, 1), F32), pltpu.VMEM((2, tq, 1), F32),
                        pltpu.VMEM((2, tq, A_DV), F32)],
        compiler_params=_cparams("parallel", "parallel", "arbitrary"),
        name="flash_diff_attention",
    )(lam_p, subln.reshape(1, A_DV), qkvz, qkvz, qkvz)


def _paged_kernel(pt_ref, lam_ref, g_ref, x_ref, *rest, pages, n_steps, width, lam_init):
    k_refs = rest[:pages]
    v_refs = rest[pages:2 * pages]
    o_ref, qm_ref, m_ref, l_ref, acc_ref = rest[2 * pages:]
    step = pl.program_id(1)
    rows = 2 * width // A_DV

    @pl.when(step == 0)
    def _():
        q = x_ref[0, :, 0:width] * (A_DK ** -0.5)
        r = lax.broadcasted_iota(jnp.int32, (rows, width), 0)
        lane = lax.broadcasted_iota(jnp.int32, (rows, width), 1)
        qm_ref[...] = jnp.where(lane // A_DK == r, q, 0.0)
        m_ref[...] = jnp.full(m_ref.shape, -jnp.inf, F32)
        l_ref[...] = jnp.zeros(l_ref.shape, F32)
        acc_ref[...] = jnp.zeros(acc_ref.shape, F32)

    qm = qm_ref[...].astype(BF16)
    s = jnp.concatenate([_dot_nt(qm, k_refs[i][...].astype(BF16)) for i in range(pages)], axis=-1)
    m_prev = m_ref[...]
    m_new = jnp.maximum(m_prev, jnp.max(s, axis=-1, keepdims=True))
    alpha = jnp.exp(m_prev - m_new)
    p = jnp.exp(s - m_new)
    l_ref[...] = alpha * l_ref[...] + jnp.sum(p, axis=-1, keepdims=True)
    p = p.astype(BF16)
    page = k_refs[0].shape[0]
    pv = _dot(p[:, 0:page], v_refs[0][...].astype(BF16))
    for i in range(1, pages):
        pv = pv + _dot(p[:, i * page:(i + 1) * page], v_refs[i][...].astype(BF16))
    acc_ref[...] = alpha * acc_ref[...] + pv
    m_ref[...] = m_new

    @pl.when(step == n_steps - 1)
    def _():
        k_cur = x_ref[0, :, width:2 * width]
        v_cur = x_ref[0, :, 2 * width:3 * width]
        s_cur = jnp.sum(qm_ref[...] * k_cur, axis=-1, keepdims=True)
        m_fin = jnp.maximum(m_ref[...], s_cur)
        a_fin = jnp.exp(m_ref[...] - m_fin)
        p_cur = jnp.exp(s_cur - m_fin)
        l_fin = a_fin * l_ref[...] + p_cur
        o_all = (a_fin * acc_ref[...] + p_cur * v_cur) / l_fin
        r = lax.broadcasted_iota(jnp.int32, (rows, width), 0)
        lane = lax.broadcasted_iota(jnp.int32, (rows, width), 1)
        own = lane // A_DV == r // 2
        o1 = jnp.sum(jnp.where(own & (r % 2 == 0), o_all, 0.0), axis=0, keepdims=True)
        o2 = jnp.sum(jnp.where(own & (r % 2 == 1), o_all, 0.0), axis=0, keepdims=True)
        o = o1 - _lam(lam_ref, lam_init) * o2
        heads = width // A_DV
        hr = lax.broadcasted_iota(jnp.int32, (heads, width), 0)
        hl = lax.broadcasted_iota(jnp.int32, (heads, width), 1)
        mine = hl // A_DV == hr
        ms = jnp.sum(jnp.where(mine, o * o, 0.0), axis=-1, keepdims=True) * (1.0 / A_DV)
        inv = jnp.sum(jnp.where(mine, lax.rsqrt(ms + SUBLN_EPS), 0.0), axis=0, keepdims=True)
        o_ref[0] = o * inv * g_ref[...] * (1.0 - lam_init)


def paged_diff_attention(qkvz, cache_k, cache_v, page_table, layer, lam_p, subln, lam_init, pages=4):
    bsz, _, four_w = qkvz.shape
    width = four_w // 4
    n_pages = page_table.shape[1]
    pages = math.gcd(pages, n_pages)
    n_steps = n_pages // pages
    page_size = cache_k.shape[2]
    ck = cache_k.reshape(cache_k.shape[0], cache_k.shape[1], page_size, width)
    cv = cache_v.reshape(cache_v.shape[0], cache_v.shape[1], page_size, width)
    rows = 2 * width // A_DV

    def page_spec(i):
        return pl.BlockSpec((None, None, page_size, width),
                            lambda b, p, pt: (layer, pt[b, p * pages + i], 0, 0))

    kern = functools.partial(_paged_kernel, pages=pages, n_steps=n_steps, width=width, lam_init=lam_init)
    grid_spec = pltpu.PrefetchScalarGridSpec(
        num_scalar_prefetch=1,
        grid=(bsz, n_steps),
        in_specs=[pl.BlockSpec((4, A_DK), lambda b, p, pt: (0, 0)),
                  pl.BlockSpec((1, width), lambda b, p, pt: (0, 0)),
                  pl.BlockSpec((1, 1, four_w), lambda b, p, pt: (b, 0, 0))]
                 + [page_spec(i) for i in range(pages)] + [page_spec(i) for i in range(pages)],
        out_specs=pl.BlockSpec((1, 1, width), lambda b, p, pt: (b, 0, 0)),
        scratch_shapes=[pltpu.VMEM((rows, width), F32), pltpu.VMEM((rows, 1), F32),
                        pltpu.VMEM((rows, 1), F32), pltpu.VMEM((rows, width), F32)])
    return pl.pallas_call(
        kern,
        grid_spec=grid_spec,
        out_shape=jax.ShapeDtypeStruct((bsz, 1, width), F32),
        compiler_params=_cparams("parallel", "arbitrary"),
        name="paged_diff_attention",
    )(page_table, lam_p, jnp.tile(subln, width // A_DV).reshape(1, width), qkvz,
      *([ck] * pages), *([cv] * pages))


def _rwkv_pre_kernel(*refs, has_vmix):
    (hn_ref, prev_ref, mix_ref, wrkvz_ref, w0_ref, w1_ref, w2_ref, a0_ref, a1_ref, a2_ref) = refs[:10]
    refs = refs[10:]
    if has_vmix:
        v0_ref, v1_ref, v2_ref, vf_ref = refs[:4]
        refs = refs[4:]
    r_ref, w_ref, k_ref, v_ref, a_ref, z_ref = refs

    hn = hn_ref[...]
    delta = prev_ref[...] - hn

    def mixed(m):
        return (hn + delta * mix_ref[m:m + 1, :]).astype(BF16)

    xv = mixed(2)
    r_ref[...] = _dot(mixed(0), wrkvz_ref[0])
    k_ref[...] = _dot(mixed(1), wrkvz_ref[1])
    v = _dot(xv, wrkvz_ref[2])
    z_ref[...] = _dot(mixed(3), wrkvz_ref[3])
    lw = _dot(jnp.tanh(_dot(mixed(4), w1_ref[...])).astype(BF16), w2_ref[...])
    x = w0_ref[...] + lw
    w_ref[...] = -(jnp.maximum(-x, 0.0) + jnp.log1p(jnp.exp(-jnp.abs(x)))) - 0.5
    la = _dot(_dot(mixed(5), a1_ref[...]).astype(BF16), a2_ref[...])
    a_ref[...] = _sigmoid(a0_ref[...] + la)
    if has_vmix:
        lv = _dot(_dot(xv, v1_ref[...]).astype(BF16), v2_ref[...])
        v = v + (vf_ref[...] - v) * _sigmoid(v0_ref[...] + lv)
    v_ref[...] = v


def rwkv_projections(hn, prev, mix, wrkvz, w0, w1, w2, a0, a1, a2, vmix=None):
    t, d = hn.shape
    tm = _row_tile(t, 256)
    row = pl.BlockSpec((tm, d), lambda i: (i, 0))

    def whole(x):
        return pl.BlockSpec(x.shape, lambda i: (0,) * x.ndim)

    args = [hn, prev, mix, wrkvz, w0.reshape(1, d), w1, w2, a0.reshape(1, d), a1, a2]
    in_specs = [row, row] + [whole(x) for x in args[2:]]
    if vmix is not None:
        v0, v1, v2, v_first = vmix
        extra = [v0.reshape(1, d), v1, v2]
        args += extra + [v_first]
        in_specs += [whole(x) for x in extra] + [row]
    return pl.pallas_call(
        functools.partial(_rwkv_pre_kernel, has_vmix=vmix is not None),
        grid=(t // tm,),
        in_specs=in_specs,
        out_specs=[row] * 6,
        out_shape=[jax.ShapeDtypeStruct((t, d), F32)] * 6,
        compiler_params=_cparams("parallel"),
        name="rwkv_projections",
    )(*args)


def _scan_kernel(r_ref, w_ref, k_ref, a_ref, v_ref, s0_ref, kkb_ref, ka_ref, rk_ref, lng_ref, lnb_ref,
                 o_ref, sfin_ref, st_ref, ob_ref, *, tb, n_tb):
    tblk = pl.program_id(1)

    @pl.when(tblk == 0)
    def _():
        st_ref[...] = s0_ref[0]

    def step(t, carry):
        r = r_ref[0, t]
        k = k_ref[0, t]
        a = a_ref[0, t]
        decay = jnp.exp(-jnp.exp(w_ref[0, t]))
        kk = k * kkb_ref[...]
        kk = kk / jnp.maximum(jnp.sqrt(jnp.sum(kk * kk, axis=0, keepdims=True)), 1e-12)
        k = k * (1.0 + (a - 1.0) * ka_ref[...])
        neg_kk = -kk
        kka = kk * a
        for i in range(SCAN_ROWS):
            s = st_ref[i]
            sa = jnp.sum(s * neg_kk, axis=0, keepdims=True)
            s = s * decay + sa * kka + v_ref[0, t, pl.ds(i, 1), :] * k
            st_ref[i] = s
            ob_ref[pl.ds(i, 1), :] = jnp.sum(s * r, axis=0, keepdims=True)
        o = ob_ref[...]

        def head_mean(x):
            tot = jnp.sum(x, axis=0, keepdims=True)
            return (tot + pltpu.roll(tot, SCAN_HEADS, axis=1)) * (1.0 / B_HEAD)

        dev = o - head_mean(o)
        o = dev * lax.rsqrt(head_mean(dev * dev) + LNX_EPS) * lng_ref[...] + lnb_ref[...]
        bonus = jnp.sum(r * k * rk_ref[...], axis=0, keepdims=True)
        o_ref[0, t] = o + bonus * v_ref[0, t]
        return carry

    lax.fori_loop(0, tb, step, 0)

    @pl.when(tblk == n_tb - 1)
    def _():
        sfin_ref[0] = st_ref[...]


def _scan_lane_param(p, heads):
    return jnp.tile(p.reshape(heads, B_HEAD).T, (1, LANES // heads))


def _scan_row_param(p, heads):
    x = p.reshape(heads, 2, SCAN_ROWS).transpose(2, 1, 0)
    x = jnp.broadcast_to(x[:, :, None, :], (SCAN_ROWS, 2, SCAN_HEADS // heads, heads))
    return x.reshape(SCAN_ROWS, LANES)


def wkv_scan(r, w, k, v, a, s0, kk_b, ka_b, rk_b, lnx_g, lnx_b, bsz, seq, tb_cap=32):
    d = r.shape[1]
    heads = d // B_HEAD
    bl = SCAN_HEADS // heads
    assert SCAN_HEADS % heads == 0 and bsz % bl == 0
    nblk = bsz // bl
    tb = _row_tile(seq, tb_cap)
    n_tb = seq // tb

    def key_layout(x):
        x = x.reshape(nblk, bl, seq, heads, B_HEAD).transpose(0, 2, 4, 1, 3)
        x = x.reshape(nblk, seq, B_HEAD, SCAN_HEADS)
        return jnp.concatenate([x, x], axis=-1)

    def val_layout(x):
        x = x.reshape(nblk, bl, seq, heads, 2, SCAN_ROWS).transpose(0, 2, 5, 4, 1, 3)
        return x.reshape(nblk, seq, SCAN_ROWS, LANES)

    st0 = s0.astype(F32).reshape(nblk, bl, heads, 2, SCAN_ROWS, B_HEAD).transpose(0, 4, 5, 3, 1, 2)
    st0 = st0.reshape(nblk, SCAN_ROWS, B_HEAD, LANES)

    key_spec = pl.BlockSpec((1, tb, B_HEAD, LANES), lambda n, t: (n, t, 0, 0))
    val_spec = pl.BlockSpec((1, tb, SCAN_ROWS, LANES), lambda n, t: (n, t, 0, 0))
    st_spec = pl.BlockSpec((1, SCAN_ROWS, B_HEAD, LANES), lambda n, t: (n, 0, 0, 0))
    kp_spec = pl.BlockSpec((B_HEAD, LANES), lambda n, t: (0, 0))
    vp_spec = pl.BlockSpec((SCAN_ROWS, LANES), lambda n, t: (0, 0))
    o, sfin = pl.pallas_call(
        functools.partial(_scan_kernel, tb=tb, n_tb=n_tb),
        grid=(nblk, n_tb),
        in_specs=[key_spec] * 4 + [val_spec, st_spec] + [kp_spec] * 3 + [vp_spec] * 2,
        out_specs=[val_spec, st_spec],
        out_shape=[jax.ShapeDtypeStruct((nblk, seq, SCAN_ROWS, LANES), F32),
                   jax.ShapeDtypeStruct((nblk, SCAN_ROWS, B_HEAD, LANES), F32)],
        scratch_shapes=[pltpu.VMEM((SCAN_ROWS, B_HEAD, LANES), F32), pltpu.VMEM((SCAN_ROWS, LANES), F32)],
        compiler_params=_cparams("parallel", "arbitrary"),
        name="wkv_scan",
    )(key_layout(r), key_layout(w), key_layout(k), key_layout(a), val_layout(v), st0,
      _scan_lane_param(kk_b, heads), _scan_lane_param(ka_b, heads), _scan_lane_param(rk_b.reshape(-1), heads),
      _scan_row_param(lnx_g, heads), _scan_row_param(lnx_b, heads))
    o = o.reshape(nblk, seq, SCAN_ROWS, 2, bl, heads).transpose(0, 4, 1, 5, 3, 2).reshape(bsz * seq, d)
    sfin = sfin.reshape(nblk, SCAN_ROWS, B_HEAD, 2, bl, heads).transpose(0, 4, 5, 3, 1, 2)
    return o, sfin.reshape(bsz, heads, B_HEAD, B_HEAD)


def _post_kernel(*refs, final):
    h_ref, o_ref, z_ref, wout_ref, p_ref, wpe_ref, wpg_ref = refs[:7]
    refs = refs[7:]
    if final:
        nf_ref, out_ref = refs
    else:
        (out_ref,) = refs
    z = z_ref[...]
    x = (o_ref[...] * (z * _sigmoid(z))).astype(BF16)
    h = h_ref[...] + _dot(x, wout_ref[...])
    gate = _sigmoid(_dot(_rms(h, NORM_EPS).astype(BF16), wpg_ref[...]))
    h = h + _dot(p_ref[...].astype(BF16), wpe_ref[...]) * gate
    if final:
        h = _rms(h, NORM_EPS) * nf_ref[...]
    out_ref[...] = h


def mixer_output(h, o, z_src, z_col, w_out, p, w_pe, w_pg, norm_f=None):
    t, d = h.shape
    tm = _row_tile(t, 256)
    row = pl.BlockSpec((tm, d), lambda i: (i, 0))

    def whole(x):
        return pl.BlockSpec(x.shape, lambda i: (0,) * x.ndim)

    args = [h, o, z_src, w_out, p, w_pe, w_pg]
    in_specs = [row, row, pl.BlockSpec((tm, d), lambda i: (i, z_col)), whole(w_out),
                pl.BlockSpec((tm, p.shape[1]), lambda i: (i, 0)), whole(w_pe), whole(w_pg)]
    if norm_f is not None:
        args.append(norm_f.reshape(1, d))
        in_specs.append(whole(args[-1]))
    return pl.pallas_call(
        functools.partial(_post_kernel, final=norm_f is not None),
        grid=(t // tm,),
        in_specs=in_specs,
        out_specs=row,
        out_shape=jax.ShapeDtypeStruct((t, d), F32),
        compiler_params=_cparams("parallel"),
        name="mixer_output",
    )(*args)


def _run_trunk(x, p, wts, paged=None, wkv_init=None, shift_init=None):
    bsz, seq, d = x.shape
    t = bsz * seq
    depth = p.shape[0]
    n_a_heads = d // A_DV
    h = x.reshape(t, d)
    v_first = None
    new_k, new_v, new_wkv, new_shift = [], [], [], []
    for i in range(depth):
        j = i // N_MIXERS
        last = i == depth - 1
        if i % N_MIXERS == 0:
            lam_init = 0.8 - 0.6 * math.exp(-0.3 * i)
            qkvz = norm_matmul(h, wts['norm_g'][i], wts['w_in_a'][j])
            new_k.append(qkvz[:, d:2 * d].reshape(bsz, seq, n_a_heads, 2, A_DK))
            new_v.append(qkvz[:, 2 * d:3 * d].reshape(bsz, seq, n_a_heads, A_DV))
            qkvz3 = qkvz.reshape(bsz, seq, 4 * d)
            if paged is None:
                o = flash_diff_attention(qkvz3, wts['lam_a'][j], wts['subln_a'][j], lam_init, n_a_heads)
            else:
                ck, cv, pt = paged
                o = paged_diff_attention(qkvz3, ck, cv, pt, j, wts['lam_a'][j], wts['subln_a'][j], lam_init)
            o = o.reshape(t, d)
            z_src, z_col, w_out = qkvz, 3, wts['w_out_a'][j]
        else:
            hn = rms_norm_call(h, wts['norm_g'][i])
            hn3 = hn.reshape(bsz, seq, d)
            if wkv_init is None:
                s0 = jnp.zeros((bsz, d // B_HEAD, B_HEAD, B_HEAD), F32)
                sh0 = jnp.zeros((bsz, d), F32)
            else:
                s0, sh0 = wkv_init[j], shift_init[j]
            prev = jnp.concatenate([sh0[:, None, :], hn3[:, :-1]], axis=1).reshape(t, d)
            vmix = None
            if v_first is not None:
                vmix = (wts['v0_b'][j - 1], wts['v1_b'][j - 1], wts['v2_b'][j - 1], v_first)
            r, w, k, v, a, z = rwkv_projections(
                hn, prev, wts['mix_b'][j], wts['w_rkvz_b'][j], wts['w0_b'][j], wts['w1_b'][j], wts['w2_b'][j],
                wts['a0_b'][j], wts['a1_b'][j], wts['a2_b'][j], vmix)
            if v_first is None:
                v_first = v
            o, s_fin = wkv_scan(r, w, k, v, a, s0, wts['kk_b'][j], wts['ka_b'][j], wts['rk_b'][j],
                                wts['lnx_g_b'][j], wts['lnx_b_b'][j], bsz, seq)
            new_wkv.append(s_fin)
            new_shift.append(hn3[:, -1])
            z_src, z_col, w_out = z, 0, wts['w_out_b'][j]
        h = mixer_output(h, o, z_src, z_col, w_out, p[i].reshape(t, -1), wts['w_pe'][i], wts['w_pg'][i],
                         wts['norm_f'] if last else None)
    return (h.reshape(bsz, seq, d), jnp.stack(new_k), jnp.stack(new_v), jnp.stack(new_wkv),
            jnp.stack(new_shift))


_MATMUL_WEIGHTS = ('w_pe', 'w_pg', 'w_in_a', 'w_out_a', 'w_rkvz_b', 'w1_b', 'w2_b', 'a1_b', 'a2_b',
                   'v1_b', 'v2_b', 'w_out_b')


def kernel(x_prompt, x_sample, cache_k, cache_v, state_wkv, state_shift, page_table, p_prompt, p_sample, norm_g, norm_f, w_pe, w_pg, w_in_a, w_out_a, lam_a, subln_a, mix_b, w_rkvz_b, w0_b, w1_b, w2_b, a0_b, a1_b, a2_b, v0_b, v1_b, v2_b, kk_b, ka_b, rk_b, lnx_g_b, lnx_b_b, w_out_b):
    wts = dict(norm_g=norm_g, norm_f=norm_f, w_pe=w_pe, w_pg=w_pg, w_in_a=w_in_a,
               w_out_a=w_out_a, lam_a=lam_a, subln_a=subln_a, mix_b=mix_b, w_rkvz_b=w_rkvz_b,
               w0_b=w0_b, w1_b=w1_b, w2_b=w2_b, a0_b=a0_b, a1_b=a1_b, a2_b=a2_b,
               v0_b=v0_b, v1_b=v1_b, v2_b=v2_b, kk_b=kk_b, ka_b=ka_b, rk_b=rk_b,
               lnx_g_b=lnx_g_b, lnx_b_b=lnx_b_b, w_out_b=w_out_b)
    for name in _MATMUL_WEIGHTS:
        wts[name] = wts[name].astype(BF16)
    y_p, k_p, v_p, wkv_p, sh_p = _run_trunk(x_prompt, p_prompt, wts)
    y_s, k_s, v_s, wkv_s, sh_s = _run_trunk(x_sample, p_sample, wts, paged=(cache_k, cache_v, page_table),
                                            wkv_init=state_wkv, shift_init=state_shift)
    return (y_p, y_s, k_p, v_p, wkv_p, sh_p, k_s, v_s, wkv_s, sh_s)
```

```python
import functools
import math

import jax
import jax.numpy as jnp
from jax import lax
from jax.experimental import pallas as pl
from jax.experimental.pallas import tpu as pltpu

F32 = jnp.float32
BF16 = jnp.bfloat16

NORM_EPS = 1e-6
SUBLN_EPS = 1e-5
LNX_EPS = 64e-5
N_MIXERS = 2

A_DK = 64
A_DV = 2 * A_DK
B_HEAD = 64
LANES = 128
SCAN_HEADS = LANES // 2
SCAN_ROWS = B_HEAD // 2
VMEM_LIMIT = 48 * 1024 * 1024


def _cparams(*sem):
    return pltpu.CompilerParams(dimension_semantics=sem, vmem_limit_bytes=VMEM_LIMIT)


def _row_tile(t, cap):
    tile = min(t, cap)
    assert t % tile == 0, (t, tile)
    return tile


def _rms(x, eps):
    return x * lax.rsqrt(jnp.mean(x * x, axis=-1, keepdims=True) + eps)


def _sigmoid(x):
    return 1.0 / (1.0 + jnp.exp(-x))


def _dot(a, b):
    return jnp.dot(a, b, preferred_element_type=F32)


def _dot_nt(a, b):
    return lax.dot_general(a, b, (((1,), (1,)), ((), ())), preferred_element_type=F32)


def _norm_kernel(x_ref, g_ref, o_ref):
    o_ref[...] = _rms(x_ref[...], NORM_EPS) * g_ref[...]


def rms_norm_call(x, g):
    t, d = x.shape
    tm = _row_tile(t, 512)
    return pl.pallas_call(
        _norm_kernel,
        grid=(t // tm,),
        in_specs=[pl.BlockSpec((tm, d), lambda i: (i, 0)), pl.BlockSpec((1, d), lambda i: (0, 0))],
        out_specs=pl.BlockSpec((tm, d), lambda i: (i, 0)),
        out_shape=jax.ShapeDtypeStruct((t, d), F32),
        compiler_params=_cparams("parallel"),
        name="rms_norm",
    )(x, g.reshape(1, d))


def _norm_mm_kernel(x_ref, g_ref, w_ref, o_ref, xn_ref):
    @pl.when(pl.program_id(1) == 0)
    def _():
        xn_ref[...] = (_rms(x_ref[...], NORM_EPS) * g_ref[...]).astype(BF16)

    o_ref[...] = _dot(xn_ref[...], w_ref[...])


def norm_matmul(x, g, w):
    t, d = x.shape
    n = w.shape[1]
    tm = _row_tile(t, 512)
    tn = _row_tile(n, 1024)
    return pl.pallas_call(
        _norm_mm_kernel,
        grid=(t // tm, n // tn),
        in_specs=[pl.BlockSpec((tm, d), lambda i, j: (i, 0)),
                  pl.BlockSpec((1, d), lambda i, j: (0, 0)),
                  pl.BlockSpec((d, tn), lambda i, j: (0, j))],
        out_specs=pl.BlockSpec((tm, tn), lambda i, j: (i, j)),
        out_shape=jax.ShapeDtypeStruct((t, n), F32),
        scratch_shapes=[pltpu.VMEM((tm, d), BF16)],
        compiler_params=_cparams("parallel", "arbitrary"),
        name="norm_matmul",
    )(x, g.reshape(1, d), w)


def _lam(lam_ref, lam_init):
    lp = lam_ref[...]
    s1 = jnp.sum(lp[0:1] * lp[1:2], axis=-1, keepdims=True)
    s2 = jnp.sum(lp[2:3] * lp[3:4], axis=-1, keepdims=True)
    return jnp.exp(s1) - jnp.exp(s2) + lam_init


def _flash_kernel(lam_ref, g_ref, q_ref, k_ref, v_ref, o_ref, kb_ref, vt_ref, m_ref, l_ref, acc_ref,
                  sa_ref, sb_ref, *, tq, lam_init):
    qi = pl.program_id(2)
    seq = k_ref.shape[1]

    @pl.when(qi == 0)
    def _():
        kb_ref[...] = k_ref[0].astype(BF16)
        for c in range(seq // LANES):
            sl = slice(c * LANES, (c + 1) * LANES)
            vt_ref[:, sl] = v_ref[0, sl, :].T.astype(BF16)

    qt = q_ref[0].T * (A_DK ** -0.5)
    chan = lax.broadcasted_iota(jnp.int32, qt.shape, 0)
    qt2 = jnp.concatenate([jnp.where(chan < A_DK, qt, 0.0), jnp.where(chan >= A_DK, qt, 0.0)],
                          axis=1).astype(BF16)
    m_ref[...] = jnp.full(m_ref.shape, -jnp.inf, F32)
    l_ref[...] = jnp.zeros(l_ref.shape, F32)
    acc_ref[...] = jnp.zeros(acc_ref.shape, F32)

    def scores(kb):
        start = pl.multiple_of(kb * tq, tq)
        return _dot(kb_ref[pl.ds(start, tq), :], qt2)

    def update(kb, s):
        start = pl.multiple_of(kb * tq, tq)
        m_prev = m_ref[...]
        m_new = jnp.maximum(m_prev, jnp.max(s, axis=0, keepdims=True))
        alpha = jnp.exp(m_prev - m_new)
        p = jnp.exp(s - m_new)
        l_ref[...] = alpha * l_ref[...] + jnp.sum(p, axis=0, keepdims=True)
        acc_ref[...] = alpha * acc_ref[...] + _dot(vt_ref[:, pl.ds(start, tq)], p.astype(BF16))
        m_ref[...] = m_new

    def diagonal(s_buf):
        s = s_buf[...]
        key = lax.broadcasted_iota(jnp.int32, s.shape, 0)
        qry = lax.broadcasted_iota(jnp.int32, s.shape, 1)
        qry = jnp.where(qry >= tq, qry - tq, qry)
        update(qi, jnp.where(key <= qry, s, -jnp.inf))

    sa_ref[...] = scores(0)

    def pair(i, carry):
        kb = 2 * i
        sb_ref[...] = scores(kb + 1)
        update(kb, sa_ref[...])
        sa_ref[...] = scores(kb + 2)
        update(kb + 1, sb_ref[...])
        return carry

    lax.fori_loop(0, qi // 2, pair, 0)

    @pl.when(qi % 2 == 0)
    def _():
        diagonal(sa_ref)

    @pl.when(qi % 2 == 1)
    def _():
        sb_ref[...] = scores(qi)
        update(qi - 1, sa_ref[...])
        diagonal(sb_ref)

    lam = _lam(lam_ref, lam_init)
    o = acc_ref[:, 0:tq] / l_ref[:, 0:tq] - lam * (acc_ref[:, tq:2 * tq] / l_ref[:, tq:2 * tq])
    o = o * lax.rsqrt(jnp.mean(o * o, axis=0, keepdims=True) + SUBLN_EPS)
    o_ref[0] = (o * (g_ref[...] * (1.0 - lam_init))).T


def flash_diff_attention(qkvz, lam_p, subln, lam_init, n_heads, tq_cap=512):
    b, s, _ = qkvz.shape
    tq = _row_tile(s, tq_cap)
    assert s % LANES == 0 and tq % LANES == 0
    kern = functools.partial(_flash_kernel, tq=tq, lam_init=lam_init)
    return pl.pallas_call(
        kern,
        grid=(b, n_heads, s // tq),
        in_specs=[pl.BlockSpec((4, A_DK), lambda bi, h, qi: (0, 0)),
                  pl.BlockSpec((A_DV, 1), lambda bi, h, qi: (0, 0)),
                  pl.BlockSpec((1, tq, A_DV), lambda bi, h, qi: (bi, qi, h)),
                  pl.BlockSpec((1, s, A_DV), lambda bi, h, qi: (bi, 0, n_heads + h)),
                  pl.BlockSpec((1, s, A_DV), lambda bi, h, qi: (bi, 0, 2 * n_heads + h))],
        out_specs=pl.BlockSpec((1, tq, A_DV), lambda bi, h, qi: (bi, qi, h)),
        out_shape=jax.ShapeDtypeStruct((b, s, n_heads * A_DV), F32),
        scratch_shapes=[pltpu.VMEM((s, A_DV), BF16), pltpu.VMEM((A_DV, s), BF16),
                        pltpu.VMEM((1, 2 * tq), F32), pltpu.VMEM((1, 2 * tq), F32),
                        pltpu.VMEM((A_DV, 2 * tq), F32), pltpu.VMEM((tq, 2 * tq), F32),
                        pltpu.VMEM((tq, 2 * tq), F32)],
        compiler_params=_cparams("parallel", "parallel", "arbitrary"),
        name="flash_diff_attention",
    )(lam_p, subln.reshape(A_DV, 1), qkvz, qkvz, qkvz)


def _paged_kernel(pt_ref, lam_ref, g_ref, x_ref, *rest, pages, n_steps, width, lam_init):
    kt_refs = rest[:pages]
    v_refs = rest[pages:2 * pages]
    o_ref, qm_ref, m_ref, l_ref, acc_ref, fin_ref = rest[2 * pages:]
    step = pl.program_id(1)
    heads = width // A_DV
    rows = 2 * heads
    page = kt_refs[0].shape[1]
    head_of_row = lax.broadcasted_iota(jnp.int32, (rows, A_DV), 0) // 2

    @pl.when(step == 0)
    def _():
        q = x_ref[0, :, 0:width] * (A_DK ** -0.5)
        r = lax.broadcasted_iota(jnp.int32, (rows, width), 0)
        lane = lax.broadcasted_iota(jnp.int32, (rows, width), 1)
        qm_ref[...] = jnp.where(lane // A_DK == r, q, 0.0)
        m_ref[...] = jnp.full(m_ref.shape, -jnp.inf, F32)
        l_ref[...] = jnp.zeros(l_ref.shape, F32)
        acc_ref[...] = jnp.zeros(acc_ref.shape, F32)

    qm = qm_ref[...].astype(BF16)
    s = jnp.concatenate([_dot(qm, kt_refs[i][...].astype(BF16)) for i in range(pages)], axis=-1)
    m_prev = m_ref[...]
    m_new = jnp.maximum(m_prev, jnp.max(s, axis=-1, keepdims=True))
    alpha = jnp.exp(m_prev - m_new)
    p = jnp.exp(s - m_new)
    l_ref[...] = alpha * l_ref[...] + jnp.sum(p, axis=-1, keepdims=True)
    p = p.astype(BF16)
    pv = jnp.zeros((rows, A_DV), F32)
    for i in range(pages):
        pi = p[:, i * page:(i + 1) * page]
        for h in range(heads):
            vh = v_refs[i][pl.ds(h, page, stride=heads), :].astype(BF16)
            pv = pv + jnp.where(head_of_row == h, _dot(pi, vh), 0.0)
    acc_ref[...] = alpha * acc_ref[...] + pv
    m_ref[...] = m_new

    @pl.when(step == n_steps - 1)
    def _():
        k_cur = x_ref[0, :, width:2 * width]
        s_cur = jnp.sum(qm_ref[...] * k_cur, axis=-1, keepdims=True)
        v_cur = jnp.zeros((rows, A_DV), F32)
        for h in range(heads):
            vh = x_ref[0, :, 2 * width + h * A_DV:2 * width + (h + 1) * A_DV]
            v_cur = jnp.where(head_of_row == h, vh, v_cur)
        m_fin = jnp.maximum(m_ref[...], s_cur)
        a_fin = jnp.exp(m_ref[...] - m_fin)
        p_cur = jnp.exp(s_cur - m_fin)
        l_fin = a_fin * l_ref[...] + p_cur
        fin_ref[...] = (a_fin * acc_ref[...] + p_cur * v_cur) / l_fin
        o = fin_ref[pl.ds(0, heads, stride=2), :] - _lam(lam_ref, lam_init) * fin_ref[pl.ds(1, heads, stride=2), :]
        o_ref[0] = _rms(o, SUBLN_EPS) * g_ref[...] * (1.0 - lam_init)


def paged_diff_attention(qkvz, cache_k, cache_v, page_table, layer, lam_p, subln, lam_init, pages=4):
    bsz, _, four_w = qkvz.shape
    width = four_w // 4
    heads = width // A_DV
    n_pages = page_table.shape[1]
    pages = math.gcd(pages, n_pages)
    n_steps = n_pages // pages
    n_layers, n_phys, page_size = cache_k.shape[:3]
    ckt = jnp.transpose(cache_k, (0, 1, 3, 4, 5, 2)).reshape(n_layers, n_phys, width, page_size)
    cv = cache_v.reshape(n_layers, n_phys, page_size * heads, A_DV)
    rows = 2 * heads

    def page_spec(i, shape):
        return pl.BlockSpec((None, None) + shape, lambda b, p, pt: (layer, pt[b, p * pages + i], 0, 0))

    kern = functools.partial(_paged_kernel, pages=pages, n_steps=n_steps, width=width, lam_init=lam_init)
    grid_spec = pltpu.PrefetchScalarGridSpec(
        num_scalar_prefetch=1,
        grid=(bsz, n_steps),
        in_specs=[pl.BlockSpec((4, A_DK), lambda b, p, pt: (0, 0)),
                  pl.BlockSpec((1, A_DV), lambda b, p, pt: (0, 0)),
                  pl.BlockSpec((1, 1, four_w), lambda b, p, pt: (b, 0, 0))]
                 + [page_spec(i, (width, page_size)) for i in range(pages)]
                 + [page_spec(i, (page_size * heads, A_DV)) for i in range(pages)],
        out_specs=pl.BlockSpec((1, heads, A_DV), lambda b, p, pt: (b, 0, 0)),
        scratch_shapes=[pltpu.VMEM((rows, width), F32), pltpu.VMEM((rows, 1), F32),
                        pltpu.VMEM((rows, 1), F32), pltpu.VMEM((rows, A_DV), F32),
                        pltpu.VMEM((rows, A_DV), F32)])
    return pl.pallas_call(
        kern,
        grid_spec=grid_spec,
        out_shape=jax.ShapeDtypeStruct((bsz, heads, A_DV), F32),
        compiler_params=_cparams("parallel", "arbitrary"),
        name="paged_diff_attention",
    )(page_table, lam_p, subln.reshape(1, A_DV), qkvz, *([ckt] * pages), *([cv] * pages))


def _rwkv_pre_kernel(*refs, has_vmix):
    (hn_ref, prev_ref, mix_ref, wrkvz_ref, w0_ref, w1_ref, w2_ref, a0_ref, a1_ref, a2_ref) = refs[:10]
    refs = refs[10:]
    if has_vmix:
        v0_ref, v1_ref, v2_ref, vf_ref = refs[:4]
        refs = refs[4:]
    r_ref, w_ref, k_ref, v_ref, a_ref, z_ref = refs

    hn = hn_ref[...]
    delta = prev_ref[...] - hn

    def mixed(m):
        return (hn + delta * mix_ref[m:m + 1, :]).astype(BF16)

    xv = mixed(2)
    r_ref[...] = _dot(mixed(0), wrkvz_ref[0])
    k_ref[...] = _dot(mixed(1), wrkvz_ref[1])
    v = _dot(xv, wrkvz_ref[2])
    z_ref[...] = _dot(mixed(3), wrkvz_ref[3])
    lw = _dot(jnp.tanh(_dot(mixed(4), w1_ref[...])).astype(BF16), w2_ref[...])
    x = w0_ref[...] + lw
    w_ref[...] = -(jnp.maximum(-x, 0.0) + jnp.log1p(jnp.exp(-jnp.abs(x)))) - 0.5
    la = _dot(_dot(mixed(5), a1_ref[...]).astype(BF16), a2_ref[...])
    a_ref[...] = _sigmoid(a0_ref[...] + la)
    if has_vmix:
        lv = _dot(_dot(xv, v1_ref[...]).astype(BF16), v2_ref[...])
        v = v + (vf_ref[...] - v) * _sigmoid(v0_ref[...] + lv)
    v_ref[...] = v


def rwkv_projections(hn, prev, mix, wrkvz, w0, w1, w2, a0, a1, a2, vmix=None):
    t, d = hn.shape
    tm = _row_tile(t, 256)
    row = pl.BlockSpec((tm, d), lambda i: (i, 0))

    def whole(x):
        return pl.BlockSpec(x.shape, lambda i: (0,) * x.ndim)

    args = [hn, prev, mix, wrkvz, w0.reshape(1, d), w1, w2, a0.reshape(1, d), a1, a2]
    in_specs = [row, row] + [whole(x) for x in args[2:]]
    if vmix is not None:
        v0, v1, v2, v_first = vmix
        extra = [v0.reshape(1, d), v1, v2]
        args += extra + [v_first]
        in_specs += [whole(x) for x in extra] + [row]
    return pl.pallas_call(
        functools.partial(_rwkv_pre_kernel, has_vmix=vmix is not None),
        grid=(t // tm,),
        in_specs=in_specs,
        out_specs=[row] * 6,
        out_shape=[jax.ShapeDtypeStruct((t, d), F32)] * 6,
        compiler_params=_cparams("parallel"),
        name="rwkv_projections",
    )(*args)


def _scan_kernel(r_ref, w_ref, k_ref, a_ref, v_ref, s0_ref, kkb_ref, ka_ref, rk_ref, lng_ref, lnb_ref,
                 o_ref, sfin_ref, st_ref, ob_ref, *, tb, n_tb):
    tblk = pl.program_id(1)

    @pl.when(tblk == 0)
    def _():
        st_ref[...] = s0_ref[0]

    def step(t, carry):
        r = r_ref[0, t]
        k = k_ref[0, t]
        a = a_ref[0, t]
        decay = jnp.exp(-jnp.exp(w_ref[0, t]))
        kk = k * kkb_ref[...]
        kk = kk / jnp.maximum(jnp.sqrt(jnp.sum(kk * kk, axis=0, keepdims=True)), 1e-12)
        k = k * (1.0 + (a - 1.0) * ka_ref[...])
        neg_kk = -kk
        kka = kk * a
        for i in range(SCAN_ROWS):
            s = st_ref[i]
            sa = jnp.sum(s * neg_kk, axis=0, keepdims=True)
            s = s * decay + sa * kka + v_ref[0, t, pl.ds(i, 1), :] * k
            st_ref[i] = s
            ob_ref[pl.ds(i, 1), :] = jnp.sum(s * r, axis=0, keepdims=True)
        o = ob_ref[...]

        def head_mean(x):
            tot = jnp.sum(x, axis=0, keepdims=True)
            return (tot + pltpu.roll(tot, SCAN_HEADS, axis=1)) * (1.0 / B_HEAD)

        dev = o - head_mean(o)
        o = dev * lax.rsqrt(head_mean(dev * dev) + LNX_EPS) * lng_ref[...] + lnb_ref[...]
        bonus = jnp.sum(r * k * rk_ref[...], axis=0, keepdims=True)
        o_ref[0, t] = o + bonus * v_ref[0, t]
        return carry

    lax.fori_loop(0, tb, step, 0)

    @pl.when(tblk == n_tb - 1)
    def _():
        sfin_ref[0] = st_ref[...]


def _scan_lane_param(p, heads):
    return jnp.tile(p.reshape(heads, B_HEAD).T, (1, LANES // heads))


def _scan_row_param(p, heads):
    x = p.reshape(heads, 2, SCAN_ROWS).transpose(2, 1, 0)
    x = jnp.broadcast_to(x[:, :, None, :], (SCAN_ROWS, 2, SCAN_HEADS // heads, heads))
    return x.reshape(SCAN_ROWS, LANES)


def wkv_scan(r, w, k, v, a, s0, kk_b, ka_b, rk_b, lnx_g, lnx_b, bsz, seq, tb_cap=32):
    d = r.shape[1]
    heads = d // B_HEAD
    bl = SCAN_HEADS // heads
    assert SCAN_HEADS % heads == 0 and bsz % bl == 0
    nblk = bsz // bl
    tb = _row_tile(seq, tb_cap)
    n_tb = seq // tb

    def key_layout(x):
        x = x.reshape(nblk, bl, seq, heads, B_HEAD).transpose(0, 2, 4, 1, 3)
        x = x.reshape(nblk, seq, B_HEAD, SCAN_HEADS)
        return jnp.concatenate([x, x], axis=-1)

    def val_layout(x):
        x = x.reshape(nblk, bl, seq, heads, 2, SCAN_ROWS).transpose(0, 2, 5, 4, 1, 3)
        return x.reshape(nblk, seq, SCAN_ROWS, LANES)

    st0 = s0.astype(F32).reshape(nblk, bl, heads, 2, SCAN_ROWS, B_HEAD).transpose(0, 4, 5, 3, 1, 2)
    st0 = st0.reshape(nblk, SCAN_ROWS, B_HEAD, LANES)

    key_spec = pl.BlockSpec((1, tb, B_HEAD, LANES), lambda n, t: (n, t, 0, 0))
    val_spec = pl.BlockSpec((1, tb, SCAN_ROWS, LANES), lambda n, t: (n, t, 0, 0))
    st_spec = pl.BlockSpec((1, SCAN_ROWS, B_HEAD, LANES), lambda n, t: (n, 0, 0, 0))
    kp_spec = pl.BlockSpec((B_HEAD, LANES), lambda n, t: (0, 0))
    vp_spec = pl.BlockSpec((SCAN_ROWS, LANES), lambda n, t: (0, 0))
    o, sfin = pl.pallas_call(
        functools.partial(_scan_kernel, tb=tb, n_tb=n_tb),
        grid=(nblk, n_tb),
        in_specs=[key_spec] * 4 + [val_spec, st_spec] + [kp_spec] * 3 + [vp_spec] * 2,
        out_specs=[val_spec, st_spec],
        out_shape=[jax.ShapeDtypeStruct((nblk, seq, SCAN_ROWS, LANES), F32),
                   jax.ShapeDtypeStruct((nblk, SCAN_ROWS, B_HEAD, LANES), F32)],
        scratch_shapes=[pltpu.VMEM((SCAN_ROWS, B_HEAD, LANES), F32), pltpu.VMEM((SCAN_ROWS, LANES), F32)],
        compiler_params=_cparams("parallel", "arbitrary"),
        name="wkv_scan",
    )(key_layout(r), key_layout(w), key_layout(k), key_layout(a), val_layout(v), st0,
      _scan_lane_param(kk_b, heads), _scan_lane_param(ka_b, heads), _scan_lane_param(rk_b.reshape(-1), heads),
      _scan_row_param(lnx_g, heads), _scan_row_param(lnx_b, heads))
    o = o.reshape(nblk, seq, SCAN_ROWS, 2, bl, heads).transpose(0, 4, 1, 5, 3, 2).reshape(bsz * seq, d)
    sfin = sfin.reshape(nblk, SCAN_ROWS, B_HEAD, 2, bl, heads).transpose(0, 4, 5, 3, 1, 2)
    return o, sfin.reshape(bsz, heads, B_HEAD, B_HEAD)


def _post_kernel(*refs, final):
    h_ref, o_ref, z_ref, wout_ref, p_ref, wpe_ref, wpg_ref = refs[:7]
    refs = refs[7:]
    if final:
        nf_ref, out_ref = refs
    else:
        (out_ref,) = refs
    z = z_ref[...]
    x = (o_ref[...] * (z * _sigmoid(z))).astype(BF16)
    h = h_ref[...] + _dot(x, wout_ref[...])
    gate = _sigmoid(_dot(_rms(h, NORM_EPS).astype(BF16), wpg_ref[...]))
    h = h + _dot(p_ref[...].astype(BF16), wpe_ref[...]) * gate
    if final:
        h = _rms(h, NORM_EPS) * nf_ref[...]
    out_ref[...] = h


def mixer_output(h, o, z_src, z_col, w_out, p, w_pe, w_pg, norm_f=None):
    t, d = h.shape
    tm = _row_tile(t, 256)
    row = pl.BlockSpec((tm, d), lambda i: (i, 0))

    def whole(x):
        return pl.BlockSpec(x.shape, lambda i: (0,) * x.ndim)

    args = [h, o, z_src, w_out, p, w_pe, w_pg]
    in_specs = [row, row, pl.BlockSpec((tm, d), lambda i: (i, z_col)), whole(w_out),
                pl.BlockSpec((tm, p.shape[1]), lambda i: (i, 0)), whole(w_pe), whole(w_pg)]
    if norm_f is not None:
        args.append(norm_f.reshape(1, d))
        in_specs.append(whole(args[-1]))
    return pl.pallas_call(
        functools.partial(_post_kernel, final=norm_f is not None),
        grid=(t // tm,),
        in_specs=in_specs,
        out_specs=row,
        out_shape=jax.ShapeDtypeStruct((t, d), F32),
        compiler_params=_cparams("parallel"),
        name="mixer_output",
    )(*args)


def _run_trunk(x, p, wts, paged=None, wkv_init=None, shift_init=None):
    bsz, seq, d = x.shape
    t = bsz * seq
    depth = p.shape[0]
    n_a_heads = d // A_DV
    h = x.reshape(t, d)
    v_first = None
    new_k, new_v, new_wkv, new_shift = [], [], [], []
    for i in range(depth):
        j = i // N_MIXERS
        last = i == depth - 1
        if i % N_MIXERS == 0:
            lam_init = 0.8 - 0.6 * math.exp(-0.3 * i)
            qkvz = norm_matmul(h, wts['norm_g'][i], wts['w_in_a'][j])
            new_k.append(qkvz[:, d:2 * d].reshape(bsz, seq, n_a_heads, 2, A_DK))
            new_v.append(qkvz[:, 2 * d:3 * d].reshape(bsz, seq, n_a_heads, A_DV))
            qkvz3 = qkvz.reshape(bsz, seq, 4 * d)
            if paged is None:
                o = flash_diff_attention(qkvz3, wts['lam_a'][j], wts['subln_a'][j], lam_init, n_a_heads)
            else:
                ck, cv, pt = paged
                o = paged_diff_attention(qkvz3, ck, cv, pt, j, wts['lam_a'][j], wts['subln_a'][j], lam_init)
            o = o.reshape(t, d)
            z_src, z_col, w_out = qkvz, 3, wts['w_out_a'][j]
        else:
            hn = rms_norm_call(h, wts['norm_g'][i])
            hn3 = hn.reshape(bsz, seq, d)
            if wkv_init is None:
                s0 = jnp.zeros((bsz, d // B_HEAD, B_HEAD, B_HEAD), F32)
                sh0 = jnp.zeros((bsz, d), F32)
            else:
                s0, sh0 = wkv_init[j], shift_init[j]
            prev = jnp.concatenate([sh0[:, None, :], hn3[:, :-1]], axis=1).reshape(t, d)
            vmix = None
            if v_first is not None:
                vmix = (wts['v0_b'][j - 1], wts['v1_b'][j - 1], wts['v2_b'][j - 1], v_first)
            r, w, k, v, a, z = rwkv_projections(
                hn, prev, wts['mix_b'][j], wts['w_rkvz_b'][j], wts['w0_b'][j], wts['w1_b'][j], wts['w2_b'][j],
                wts['a0_b'][j], wts['a1_b'][j], wts['a2_b'][j], vmix)
            if v_first is None:
                v_first = v
            o, s_fin = wkv_scan(r, w, k, v, a, s0, wts['kk_b'][j], wts['ka_b'][j], wts['rk_b'][j],
                                wts['lnx_g_b'][j], wts['lnx_b_b'][j], bsz, seq)
            new_wkv.append(s_fin)
            new_shift.append(hn3[:, -1])
            z_src, z_col, w_out = z, 0, wts['w_out_b'][j]
        h = mixer_output(h, o, z_src, z_col, w_out, p[i].reshape(t, -1), wts['w_pe'][i], wts['w_pg'][i],
                         wts['norm_f'] if last else None)
    return (h.reshape(bsz, seq, d), jnp.stack(new_k), jnp.stack(new_v), jnp.stack(new_wkv),
            jnp.stack(new_shift))


_MATMUL_WEIGHTS = ('w_pe', 'w_pg', 'w_in_a', 'w_out_a', 'w_rkvz_b', 'w1_b', 'w2_b', 'a1_b', 'a2_b',
                   'v1_b', 'v2_b', 'w_out_b')


def kernel(x_prompt, x_sample, cache_k, cache_v, state_wkv, state_shift, page_table, p_prompt, p_sample, norm_g, norm_f, w_pe, w_pg, w_in_a, w_out_a, lam_a, subln_a, mix_b, w_rkvz_b, w0_b, w1_b, w2_b, a0_b, a1_b, a2_b, v0_b, v1_b, v2_b, kk_b, ka_b, rk_b, lnx_g_b, lnx_b_b, w_out_b):
    wts = dict(norm_g=norm_g, norm_f=norm_f, w_pe=w_pe, w_pg=w_pg, w_in_a=w_in_a,
               w_out_a=w_out_a, lam_a=lam_a, subln_a=subln_a, mix_b=mix_b, w_rkvz_b=w_rkvz_b,
               w0_b=w0_b, w1_b=w1_b, w2_b=w2_b, a0_b=a0_b, a1_b=a1_b, a2_b=a2_b,
               v0_b=v0_b, v1_b=v1_b, v2_b=v2_b, kk_b=kk_b, ka_b=ka_b, rk_b=rk_b,
               lnx_g_b=lnx_g_b, lnx_b_b=lnx_b_b, w_out_b=w_out_b)
    for name in _MATMUL_WEIGHTS:
        wts[name] = wts[name].astype(BF16)
    y_p, k_p, v_p, wkv_p, sh_p = _run_trunk(x_prompt, p_prompt, wts)
    y_s, k_s, v_s, wkv_s, sh_s = _run_trunk(x_sample, p_sample, wts, paged=(cache_k, cache_v, page_table),
                                            wkv_init=state_wkv, shift_init=state_shift)
    return (y_p, y_s, k_p, v_p, wkv_p, sh_p, k_s, v_s, wkv_s, sh_s)
```

```python
import functools
import math

import jax
import jax.numpy as jnp
from jax import lax
from jax.experimental import pallas as pl
from jax.experimental.pallas import tpu as pltpu

F32 = jnp.float32
BF16 = jnp.bfloat16

NORM_EPS = 1e-6
SUBLN_EPS = 1e-5
LNX_EPS = 64e-5
N_MIXERS = 2

A_DK = 64
A_DV = 2 * A_DK
B_HEAD = 64
LANES = 128
SCAN_HEADS = LANES // 2
SCAN_ROWS = B_HEAD // 2
VMEM_LIMIT = 48 * 1024 * 1024
CT_PITCH = B_HEAD + 8


def _cparams(*sem):
    return pltpu.CompilerParams(dimension_semantics=sem, vmem_limit_bytes=VMEM_LIMIT)


def _row_tile(t, cap):
    tile = min(t, cap)
    assert t % tile == 0, (t, tile)
    return tile


def _rms(x, eps):
    return x * lax.rsqrt(jnp.mean(x * x, axis=-1, keepdims=True) + eps)


def _sigmoid(x):
    return 1.0 / (1.0 + jnp.exp(-x))


def _dot(a, b):
    return jnp.dot(a, b, preferred_element_type=F32)


def _dot_nt(a, b):
    return lax.dot_general(a, b, (((1,), (1,)), ((), ())), preferred_element_type=F32)


def _norm_kernel(x_ref, g_ref, o_ref):
    o_ref[...] = _rms(x_ref[...], NORM_EPS) * g_ref[...]


def rms_norm_call(x, g):
    t, d = x.shape
    tm = _row_tile(t, 512)
    return pl.pallas_call(
        _norm_kernel,
        grid=(t // tm,),
        in_specs=[pl.BlockSpec((tm, d), lambda i: (i, 0)), pl.BlockSpec((1, d), lambda i: (0, 0))],
        out_specs=pl.BlockSpec((tm, d), lambda i: (i, 0)),
        out_shape=jax.ShapeDtypeStruct((t, d), F32),
        compiler_params=_cparams("parallel"),
        name="rms_norm",
    )(x, g.reshape(1, d))


def _norm_mm_kernel(x_ref, g_ref, w_ref, o_ref, xn_ref):
    @pl.when(pl.program_id(1) == 0)
    def _():
        xn_ref[...] = (_rms(x_ref[...], NORM_EPS) * g_ref[...]).astype(BF16)

    o_ref[...] = _dot(xn_ref[...], w_ref[...])


def norm_matmul(x, g, w):
    t, d = x.shape
    n = w.shape[1]
    tm = _row_tile(t, 512)
    tn = _row_tile(n, 1024)
    return pl.pallas_call(
        _norm_mm_kernel,
        grid=(t // tm, n // tn),
        in_specs=[pl.BlockSpec((tm, d), lambda i, j: (i, 0)),
                  pl.BlockSpec((1, d), lambda i, j: (0, 0)),
                  pl.BlockSpec((d, tn), lambda i, j: (0, j))],
        out_specs=pl.BlockSpec((tm, tn), lambda i, j: (i, j)),
        out_shape=jax.ShapeDtypeStruct((t, n), F32),
        scratch_shapes=[pltpu.VMEM((tm, d), BF16)],
        compiler_params=_cparams("parallel", "arbitrary"),
        name="norm_matmul",
    )(x, g.reshape(1, d), w)


def _lam(lam_ref, lam_init):
    lp = lam_ref[...]
    s1 = jnp.sum(lp[0:1] * lp[1:2], axis=-1, keepdims=True)
    s2 = jnp.sum(lp[2:3] * lp[3:4], axis=-1, keepdims=True)
    return jnp.exp(s1) - jnp.exp(s2) + lam_init


def _flash_kernel(lam_ref, g_ref, q_ref, k_ref, v_ref, o_ref, kb_ref, vt_ref, m_ref, l_ref, acc_ref,
                  sa_ref, sb_ref, *, tq, lam_init):
    qi = pl.program_id(2)
    seq = k_ref.shape[1]

    @pl.when(qi == 0)
    def _():
        kb_ref[...] = k_ref[0].astype(BF16)
        for c in range(seq // LANES):
            sl = slice(c * LANES, (c + 1) * LANES)
            vt_ref[:, sl] = v_ref[0, sl, :].T.astype(BF16)

    qt = q_ref[0].T * (A_DK ** -0.5)
    chan = lax.broadcasted_iota(jnp.int32, qt.shape, 0)
    qt2 = jnp.concatenate([jnp.where(chan < A_DK, qt, 0.0), jnp.where(chan >= A_DK, qt, 0.0)],
                          axis=1).astype(BF16)
    m_ref[...] = jnp.full(m_ref.shape, -jnp.inf, F32)
    l_ref[...] = jnp.zeros(l_ref.shape, F32)
    acc_ref[...] = jnp.zeros(acc_ref.shape, F32)

    def scores(kb):
        start = pl.multiple_of(kb * tq, tq)
        return _dot(kb_ref[pl.ds(start, tq), :], qt2)

    def update(kb, s):
        start = pl.multiple_of(kb * tq, tq)
        m_prev = m_ref[...]
        m_new = jnp.maximum(m_prev, jnp.max(s, axis=0, keepdims=True))
        alpha = jnp.exp(m_prev - m_new)
        p = jnp.exp(s - m_new)
        l_ref[...] = alpha * l_ref[...] + jnp.sum(p, axis=0, keepdims=True)
        acc_ref[...] = alpha * acc_ref[...] + _dot(vt_ref[:, pl.ds(start, tq)], p.astype(BF16))
        m_ref[...] = m_new

    def diagonal(s_buf):
        s = s_buf[...]
        key = lax.broadcasted_iota(jnp.int32, s.shape, 0)
        qry = lax.broadcasted_iota(jnp.int32, s.shape, 1)
        qry = jnp.where(qry >= tq, qry - tq, qry)
        update(qi, jnp.where(key <= qry, s, -jnp.inf))

    sa_ref[...] = scores(0)

    def pair(i, carry):
        kb = 2 * i
        sb_ref[...] = scores(kb + 1)
        update(kb, sa_ref[...])
        sa_ref[...] = scores(kb + 2)
        update(kb + 1, sb_ref[...])
        return carry

    lax.fori_loop(0, qi // 2, pair, 0)

    @pl.when(qi % 2 == 0)
    def _():
        diagonal(sa_ref)

    @pl.when(qi % 2 == 1)
    def _():
        sb_ref[...] = scores(qi)
        update(qi - 1, sa_ref[...])
        diagonal(sb_ref)

    lam = _lam(lam_ref, lam_init)
    o = acc_ref[:, 0:tq] / l_ref[:, 0:tq] - lam * (acc_ref[:, tq:2 * tq] / l_ref[:, tq:2 * tq])
    o = o * lax.rsqrt(jnp.mean(o * o, axis=0, keepdims=True) + SUBLN_EPS)
    o_ref[0] = (o * (g_ref[...] * (1.0 - lam_init))).T


def flash_diff_attention(qkvz, lam_p, subln, lam_init, n_heads, tq_cap=512):
    b, s, _ = qkvz.shape
    tq = _row_tile(s, tq_cap)
    assert s % LANES == 0 and tq % LANES == 0
    kern = functools.partial(_flash_kernel, tq=tq, lam_init=lam_init)
    return pl.pallas_call(
        kern,
        grid=(b, n_heads, s // tq),
        in_specs=[pl.BlockSpec((4, A_DK), lambda bi, h, qi: (0, 0)),
                  pl.BlockSpec((A_DV, 1), lambda bi, h, qi: (0, 0)),
                  pl.BlockSpec((1, tq, A_DV), lambda bi, h, qi: (bi, qi, h)),
                  pl.BlockSpec((1, s, A_DV), lambda bi, h, qi: (bi, 0, n_heads + h)),
                  pl.BlockSpec((1, s, A_DV), lambda bi, h, qi: (bi, 0, 2 * n_heads + h))],
        out_specs=pl.BlockSpec((1, tq, A_DV), lambda bi, h, qi: (bi, qi, h)),
        out_shape=jax.ShapeDtypeStruct((b, s, n_heads * A_DV), F32),
        scratch_shapes=[pltpu.VMEM((s, A_DV), BF16), pltpu.VMEM((A_DV, s), BF16),
                        pltpu.VMEM((1, 2 * tq), F32), pltpu.VMEM((1, 2 * tq), F32),
                        pltpu.VMEM((A_DV, 2 * tq), F32), pltpu.VMEM((tq, 2 * tq), F32),
                        pltpu.VMEM((tq, 2 * tq), F32)],
        compiler_params=_cparams("parallel", "parallel", "arbitrary"),
        name="flash_diff_attention",
    )(lam_p, subln.reshape(A_DV, 1), qkvz, qkvz, qkvz)


def _paged_kernel(pt_ref, lam_ref, g_ref, x_ref, *rest, pages, n_steps, width, lam_init):
    kt_refs = rest[:pages]
    v_refs = rest[pages:2 * pages]
    o_ref, qm_ref, m_ref, l_ref, acc_ref, fin_ref = rest[2 * pages:]
    step = pl.program_id(1)
    heads = width // A_DV
    rows = 2 * heads
    page = kt_refs[0].shape[1]
    head_of_row = lax.broadcasted_iota(jnp.int32, (rows, A_DV), 0) // 2

    @pl.when(step == 0)
    def _():
        q = x_ref[0, :, 0:width] * (A_DK ** -0.5)
        r = lax.broadcasted_iota(jnp.int32, (rows, width), 0)
        lane = lax.broadcasted_iota(jnp.int32, (rows, width), 1)
        qm_ref[...] = jnp.where(lane // A_DK == r, q, 0.0)
        m_ref[...] = jnp.full(m_ref.shape, -jnp.inf, F32)
        l_ref[...] = jnp.zeros(l_ref.shape, F32)
        acc_ref[...] = jnp.zeros(acc_ref.shape, F32)

    qm = qm_ref[...].astype(BF16)
    s = jnp.concatenate([_dot(qm, kt_refs[i][...].astype(BF16)) for i in range(pages)], axis=-1)
    m_prev = m_ref[...]
    m_new = jnp.maximum(m_prev, jnp.max(s, axis=-1, keepdims=True))
    alpha = jnp.exp(m_prev - m_new)
    p = jnp.exp(s - m_new)
    l_ref[...] = alpha * l_ref[...] + jnp.sum(p, axis=-1, keepdims=True)
    p = p.astype(BF16)
    pv = jnp.zeros((rows, A_DV), F32)
    for i in range(pages):
        pi = p[:, i * page:(i + 1) * page]
        for h in range(heads):
            vh = v_refs[i][pl.ds(h, page, stride=heads), :].astype(BF16)
            pv = pv + jnp.where(head_of_row == h, _dot(pi, vh), 0.0)
    acc_ref[...] = alpha * acc_ref[...] + pv
    m_ref[...] = m_new

    @pl.when(step == n_steps - 1)
    def _():
        k_cur = x_ref[0, :, width:2 * width]
        s_cur = jnp.sum(qm_ref[...] * k_cur, axis=-1, keepdims=True)
        v_cur = jnp.zeros((rows, A_DV), F32)
        for h in range(heads):
            vh = x_ref[0, :, 2 * width + h * A_DV:2 * width + (h + 1) * A_DV]
            v_cur = jnp.where(head_of_row == h, vh, v_cur)
        m_fin = jnp.maximum(m_ref[...], s_cur)
        a_fin = jnp.exp(m_ref[...] - m_fin)
        p_cur = jnp.exp(s_cur - m_fin)
        l_fin = a_fin * l_ref[...] + p_cur
        fin_ref[...] = (a_fin * acc_ref[...] + p_cur * v_cur) / l_fin
        o = fin_ref[pl.ds(0, heads, stride=2), :] - _lam(lam_ref, lam_init) * fin_ref[pl.ds(1, heads, stride=2), :]
        o_ref[0] = _rms(o, SUBLN_EPS) * g_ref[...] * (1.0 - lam_init)


def paged_diff_attention(qkvz, cache_k, cache_v, page_table, layer, lam_p, subln, lam_init, pages=8):
    bsz, _, four_w = qkvz.shape
    width = four_w // 4
    heads = width // A_DV
    n_pages = page_table.shape[1]
    pages = math.gcd(pages, n_pages)
    n_steps = n_pages // pages
    n_layers, n_phys, page_size = cache_k.shape[:3]
    ckt = jnp.transpose(cache_k, (0, 1, 3, 4, 5, 2)).reshape(n_layers, n_phys, width, page_size)
    cv = cache_v.reshape(n_layers, n_phys, page_size * heads, A_DV)
    rows = 2 * heads

    def page_spec(i, shape):
        return pl.BlockSpec((None, None) + shape, lambda b, p, pt: (layer, pt[b, p * pages + i], 0, 0))

    kern = functools.partial(_paged_kernel, pages=pages, n_steps=n_steps, width=width, lam_init=lam_init)
    grid_spec = pltpu.PrefetchScalarGridSpec(
        num_scalar_prefetch=1,
        grid=(bsz, n_steps),
        in_specs=[pl.BlockSpec((4, A_DK), lambda b, p, pt: (0, 0)),
                  pl.BlockSpec((1, A_DV), lambda b, p, pt: (0, 0)),
                  pl.BlockSpec((1, 1, four_w), lambda b, p, pt: (b, 0, 0))]
                 + [page_spec(i, (width, page_size)) for i in range(pages)]
                 + [page_spec(i, (page_size * heads, A_DV)) for i in range(pages)],
        out_specs=pl.BlockSpec((1, heads, A_DV), lambda b, p, pt: (b, 0, 0)),
        scratch_shapes=[pltpu.VMEM((rows, width), F32), pltpu.VMEM((rows, 1), F32),
                        pltpu.VMEM((rows, 1), F32), pltpu.VMEM((rows, A_DV), F32),
                        pltpu.VMEM((rows, A_DV), F32)])
    return pl.pallas_call(
        kern,
        grid_spec=grid_spec,
        out_shape=jax.ShapeDtypeStruct((bsz, heads, A_DV), F32),
        compiler_params=_cparams("parallel", "arbitrary"),
        name="paged_diff_attention",
    )(page_table, lam_p, subln.reshape(1, A_DV), qkvz, *([ckt] * pages), *([cv] * pages))


def _rwkv_pre_kernel(*refs, has_vmix):
    (hn_ref, prev_ref, mix_ref, wrkvz_ref, w0_ref, w1_ref, w2_ref, a0_ref, a1_ref, a2_ref) = refs[:10]
    refs = refs[10:]
    if has_vmix:
        v0_ref, v1_ref, v2_ref, vf_ref = refs[:4]
        refs = refs[4:]
    r_ref, w_ref, k_ref, v_ref, a_ref, z_ref = refs

    hn = hn_ref[...]
    delta = prev_ref[...] - hn

    def mixed(m):
        return (hn + delta * mix_ref[m:m + 1, :]).astype(BF16)

    xv = mixed(2)
    r_ref[...] = _dot(mixed(0), wrkvz_ref[0])
    k_ref[...] = _dot(mixed(1), wrkvz_ref[1])
    v = _dot(xv, wrkvz_ref[2])
    z_ref[...] = _dot(mixed(3), wrkvz_ref[3])
    lw = _dot(jnp.tanh(_dot(mixed(4), w1_ref[...])).astype(BF16), w2_ref[...])
    x = w0_ref[...] + lw
    w_ref[...] = -(jnp.maximum(-x, 0.0) + jnp.log1p(jnp.exp(-jnp.abs(x)))) - 0.5
    la = _dot(_dot(mixed(5), a1_ref[...]).astype(BF16), a2_ref[...])
    a_ref[...] = _sigmoid(a0_ref[...] + la)
    if has_vmix:
        lv = _dot(_dot(xv, v1_ref[...]).astype(BF16), v2_ref[...])
        v = v + (vf_ref[...] - v) * _sigmoid(v0_ref[...] + lv)
    v_ref[...] = v


def rwkv_projections(hn, prev, mix, wrkvz, w0, w1, w2, a0, a1, a2, vmix=None):
    t, d = hn.shape
    tm = _row_tile(t, 256)
    row = pl.BlockSpec((tm, d), lambda i: (i, 0))

    def whole(x):
        return pl.BlockSpec(x.shape, lambda i: (0,) * x.ndim)

    args = [hn, prev, mix, wrkvz, w0.reshape(1, d), w1, w2, a0.reshape(1, d), a1, a2]
    in_specs = [row, row] + [whole(x) for x in args[2:]]
    if vmix is not None:
        v0, v1, v2, v_first = vmix
        extra = [v0.reshape(1, d), v1, v2]
        args += extra + [v_first]
        in_specs += [whole(x) for x in extra] + [row]
    return pl.pallas_call(
        functools.partial(_rwkv_pre_kernel, has_vmix=vmix is not None),
        grid=(t // tm,),
        in_specs=in_specs,
        out_specs=[row] * 6,
        out_shape=[jax.ShapeDtypeStruct((t, d), F32)] * 6,
        compiler_params=_cparams("parallel"),
        name="rwkv_projections",
    )(*args)


def _scan_kernel(r_ref, w_ref, k_ref, a_ref, v_ref, s0_ref, kkb_ref, ka_ref, rk_ref, lng_ref, lnb_ref,
                 o_ref, sfin_ref, st_ref, ob_ref, *, tb, n_tb, pitch):
    tblk = pl.program_id(1)

    @pl.when(tblk == 0)
    def _():
        st_ref[...] = s0_ref[0]

    def step(t, carry):
        key_rows = pl.ds(t, B_HEAD, stride=pitch)
        val_rows = pl.ds(t, SCAN_ROWS, stride=pitch)
        r = r_ref[key_rows, :]
        k = k_ref[key_rows, :]
        a = a_ref[key_rows, :]
        decay = jnp.exp(-jnp.exp(w_ref[key_rows, :]))
        kk = k * kkb_ref[...]
        kk = kk / jnp.maximum(jnp.sqrt(jnp.sum(kk * kk, axis=0, keepdims=True)), 1e-12)
        k = k * (1.0 + (a - 1.0) * ka_ref[...])
        neg_kk = -kk
        kka = kk * a
        dr = decay * r
        kka_r = jnp.sum(kka * r, axis=0, keepdims=True)
        k_r = jnp.sum(k * r, axis=0, keepdims=True)
        for i in range(SCAN_ROWS):
            s = st_ref[i]
            sa = jnp.sum(s * neg_kk, axis=0, keepdims=True)
            so = jnp.sum(s * dr, axis=0, keepdims=True)
            vi = v_ref[pl.ds(i * pitch + t, 1), :]
            st_ref[i] = s * decay + sa * kka + vi * k
            ob_ref[pl.ds(i, 1), :] = so + sa * kka_r + vi * k_r
        o = ob_ref[...]

        def head_mean(x):
            tot = jnp.sum(x, axis=0, keepdims=True)
            return (tot + pltpu.roll(tot, SCAN_HEADS, axis=1)) * (1.0 / B_HEAD)

        dev = o - head_mean(o)
        o = dev * lax.rsqrt(head_mean(dev * dev) + LNX_EPS) * lng_ref[...] + lnb_ref[...]
        bonus = jnp.sum(r * k * rk_ref[...], axis=0, keepdims=True)
        o_ref[val_rows, :] = o + bonus * v_ref[val_rows, :]
        return carry

    lax.fori_loop(0, tb, step, 0)

    @pl.when(tblk == n_tb - 1)
    def _():
        sfin_ref[0] = st_ref[...]


def _to_scan_kernel(x_ref, o_ref, ct_ref, *, rows, second, tb, pitch):
    bl, steps, d = x_ref.shape
    heads = d // B_HEAD
    pair = LANES // B_HEAD
    for b in range(bl):
        for c in range(d // LANES):
            xt = x_ref[b, :, c * LANES:(c + 1) * LANES].T
            for h in range(pair):
                ct_ref[pl.ds((b * heads + c * pair + h) * CT_PITCH, B_HEAD), :] = xt[h * B_HEAD:(h + 1) * B_HEAD]
    for r in range(rows):
        lo = ct_ref[pl.ds(r, SCAN_HEADS, stride=CT_PITCH), :]
        hi = ct_ref[pl.ds(second + r, SCAN_HEADS, stride=CT_PITCH), :]
        xt = jnp.concatenate([lo, hi], axis=0).T
        for q in range(steps // tb):
            o_ref[pl.ds((q * rows + r) * pitch, tb), :] = xt[q * tb:(q + 1) * tb]


def _from_scan_kernel(x_ref, o_ref, ct_ref, *, tb, pitch):
    bl, steps, d = o_ref.shape
    heads = d // B_HEAD
    for r in range(SCAN_ROWS):
        x = jnp.concatenate([x_ref[pl.ds((q * SCAN_ROWS + r) * pitch, tb), :] for q in range(steps // tb)], axis=0)
        xt = x.T
        for half in range(2):
            ct_ref[pl.ds((half * SCAN_ROWS + r) * CT_PITCH, SCAN_HEADS), :] = xt[half * SCAN_HEADS:(half + 1) * SCAN_HEADS]
    for b in range(bl):
        for c in range(d // LANES):
            first = b * heads + c * (LANES // B_HEAD)
            tile = jnp.concatenate([ct_ref[pl.ds(first + h, B_HEAD, stride=CT_PITCH), :]
                                    for h in range(LANES // B_HEAD)], axis=0)
            o_ref[b, :, c * LANES:(c + 1) * LANES] = tile.T


def to_scan_layout(x, bsz, seq, rows, second, tb):
    d = x.shape[1]
    pitch = _scan_pitch(tb)
    blk_rows = (LANES // tb) * rows * pitch
    out = pl.pallas_call(
        functools.partial(_to_scan_kernel, rows=rows, second=second, tb=tb, pitch=pitch),
        grid=(seq // LANES,),
        in_specs=[pl.BlockSpec((bsz, LANES, d), lambda t: (0, t, 0))],
        out_specs=pl.BlockSpec((blk_rows, LANES), lambda t: (t, 0)),
        out_shape=jax.ShapeDtypeStruct((seq // LANES * blk_rows, LANES), F32),
        scratch_shapes=[pltpu.VMEM((SCAN_HEADS * CT_PITCH, LANES), F32)],
        compiler_params=_cparams("parallel"),
        name="to_scan_layout",
    )(x.reshape(bsz, seq, d))
    return out.reshape(1, seq // tb, rows * pitch, LANES)


def from_scan_layout(o, bsz, seq, d, tb):
    pitch = _scan_pitch(tb)
    blk_rows = (LANES // tb) * SCAN_ROWS * pitch
    out = pl.pallas_call(
        functools.partial(_from_scan_kernel, tb=tb, pitch=pitch),
        grid=(seq // LANES,),
        in_specs=[pl.BlockSpec((blk_rows, LANES), lambda t: (t, 0))],
        out_specs=pl.BlockSpec((bsz, LANES, d), lambda t: (0, t, 0)),
        out_shape=jax.ShapeDtypeStruct((bsz, seq, d), F32),
        scratch_shapes=[pltpu.VMEM((B_HEAD * CT_PITCH, LANES), F32)],
        compiler_params=_cparams("parallel"),
        name="from_scan_layout",
    )(o.reshape(seq // LANES * blk_rows, LANES))
    return out.reshape(bsz * seq, d)


def _scan_pitch(tb):
    return tb if tb == 1 else tb + 8


def _scan_lane_param(p, heads):
    return jnp.tile(p.reshape(heads, B_HEAD).T, (1, LANES // heads))


def _scan_row_param(p, heads):
    x = p.reshape(heads, 2, SCAN_ROWS).transpose(2, 1, 0)
    x = jnp.broadcast_to(x[:, :, None, :], (SCAN_ROWS, 2, SCAN_HEADS // heads, heads))
    return x.reshape(SCAN_ROWS, LANES)


def wkv_scan(r, w, k, v, a, s0, kk_b, ka_b, rk_b, lnx_g, lnx_b, bsz, seq, tb_cap=32):
    d = r.shape[1]
    heads = d // B_HEAD
    bl = SCAN_HEADS // heads
    assert SCAN_HEADS % heads == 0 and bsz % bl == 0
    nblk = bsz // bl
    tb = _row_tile(seq, tb_cap)
    n_tb = seq // tb
    pitch = _scan_pitch(tb)

    def blocked(x, rows):
        x = x.reshape(nblk, n_tb, tb, rows, LANES).transpose(0, 1, 3, 2, 4)
        x = jnp.pad(x, ((0, 0), (0, 0), (0, 0), (0, pitch - tb), (0, 0)))
        return x.reshape(nblk, n_tb, rows * pitch, LANES)

    def key_layout(x):
        x = x.reshape(nblk, bl, seq, heads, B_HEAD).transpose(0, 2, 4, 1, 3)
        x = x.reshape(nblk, seq, B_HEAD, SCAN_HEADS)
        return blocked(jnp.concatenate([x, x], axis=-1), B_HEAD)

    def val_layout(x):
        x = x.reshape(nblk, bl, seq, heads, 2, SCAN_ROWS).transpose(0, 2, 5, 4, 1, 3)
        return blocked(x.reshape(nblk, seq, SCAN_ROWS, LANES), SCAN_ROWS)

    in_kernel_layout = nblk == 1 and seq % LANES == 0
    if in_kernel_layout:
        key_layout = functools.partial(to_scan_layout, bsz=bsz, seq=seq, rows=B_HEAD, second=0, tb=tb)
        val_layout = functools.partial(to_scan_layout, bsz=bsz, seq=seq, rows=SCAN_ROWS, second=SCAN_ROWS, tb=tb)

    st0 = s0.astype(F32).reshape(nblk, bl, heads, 2, SCAN_ROWS, B_HEAD).transpose(0, 4, 5, 3, 1, 2)
    st0 = st0.reshape(nblk, SCAN_ROWS, B_HEAD, LANES)

    key_spec = pl.BlockSpec((None, None, B_HEAD * pitch, LANES), lambda n, t: (n, t, 0, 0))
    val_spec = pl.BlockSpec((None, None, SCAN_ROWS * pitch, LANES), lambda n, t: (n, t, 0, 0))
    st_spec = pl.BlockSpec((1, SCAN_ROWS, B_HEAD, LANES), lambda n, t: (n, 0, 0, 0))
    kp_spec = pl.BlockSpec((B_HEAD, LANES), lambda n, t: (0, 0))
    vp_spec = pl.BlockSpec((SCAN_ROWS, LANES), lambda n, t: (0, 0))
    o, sfin = pl.pallas_call(
        functools.partial(_scan_kernel, tb=tb, n_tb=n_tb, pitch=pitch),
        grid=(nblk, n_tb),
        in_specs=[key_spec] * 4 + [val_spec, st_spec] + [kp_spec] * 3 + [vp_spec] * 2,
        out_specs=[val_spec, st_spec],
        out_shape=[jax.ShapeDtypeStruct((nblk, n_tb, SCAN_ROWS * pitch, LANES), F32),
                   jax.ShapeDtypeStruct((nblk, SCAN_ROWS, B_HEAD, LANES), F32)],
        scratch_shapes=[pltpu.VMEM((SCAN_ROWS, B_HEAD, LANES), F32), pltpu.VMEM((SCAN_ROWS, LANES), F32)],
        compiler_params=_cparams("parallel", "arbitrary"),
        name="wkv_scan",
    )(key_layout(r), key_layout(w), key_layout(k), key_layout(a), val_layout(v), st0,
      _scan_lane_param(kk_b, heads), _scan_lane_param(ka_b, heads), _scan_lane_param(rk_b.reshape(-1), heads),
      _scan_row_param(lnx_g, heads), _scan_row_param(lnx_b, heads))
    if in_kernel_layout:
        o = from_scan_layout(o, bsz, seq, d, tb)
    else:
        o = o.reshape(nblk, n_tb, SCAN_ROWS, pitch, 2, bl, heads)[:, :, :, :tb]
        o = o.transpose(0, 5, 1, 3, 6, 4, 2).reshape(bsz * seq, d)
    sfin = sfin.reshape(nblk, SCAN_ROWS, B_HEAD, 2, bl, heads).transpose(0, 4, 5, 3, 1, 2)
    return o, sfin.reshape(bsz, heads, B_HEAD, B_HEAD)


def _post_kernel(*refs, final):
    h_ref, o_ref, z_ref, wout_ref, p_ref, wpe_ref, wpg_ref = refs[:7]
    refs = refs[7:]
    if final:
        nf_ref, out_ref = refs
    else:
        (out_ref,) = refs
    z = z_ref[...]
    x = (o_ref[...] * (z * _sigmoid(z))).astype(BF16)
    h = h_ref[...] + _dot(x, wout_ref[...])
    gate = _sigmoid(_dot(_rms(h, NORM_EPS).astype(BF16), wpg_ref[...]))
    h = h + _dot(p_ref[...].astype(BF16), wpe_ref[...]) * gate
    if final:
        h = _rms(h, NORM_EPS) * nf_ref[...]
    out_ref[...] = h


def mixer_output(h, o, z_src, z_col, w_out, p, w_pe, w_pg, norm_f=None):
    t, d = h.shape
    tm = _row_tile(t, 256)
    row = pl.BlockSpec((tm, d), lambda i: (i, 0))

    def whole(x):
        return pl.BlockSpec(x.shape, lambda i: (0,) * x.ndim)

    args = [h, o, z_src, w_out, p, w_pe, w_pg]
    in_specs = [row, row, pl.BlockSpec((tm, d), lambda i: (i, z_col)), whole(w_out),
                pl.BlockSpec((tm, p.shape[1]), lambda i: (i, 0)), whole(w_pe), whole(w_pg)]
    if norm_f is not None:
        args.append(norm_f.reshape(1, d))
        in_specs.append(whole(args[-1]))
    return pl.pallas_call(
        functools.partial(_post_kernel, final=norm_f is not None),
        grid=(t // tm,),
        in_specs=in_specs,
        out_specs=row,
        out_shape=jax.ShapeDtypeStruct((t, d), F32),
        compiler_params=_cparams("parallel"),
        name="mixer_output",
    )(*args)


def _run_trunk(x, p, wts, paged=None, wkv_init=None, shift_init=None):
    bsz, seq, d = x.shape
    t = bsz * seq
    depth = p.shape[0]
    n_a_heads = d // A_DV
    h = x.reshape(t, d)
    v_first = None
    new_k, new_v, new_wkv, new_shift = [], [], [], []
    for i in range(depth):
        j = i // N_MIXERS
        last = i == depth - 1
        if i % N_MIXERS == 0:
            lam_init = 0.8 - 0.6 * math.exp(-0.3 * i)
            qkvz = norm_matmul(h, wts['norm_g'][i], wts['w_in_a'][j])
            new_k.append(qkvz[:, d:2 * d].reshape(bsz, seq, n_a_heads, 2, A_DK))
            new_v.append(qkvz[:, 2 * d:3 * d].reshape(bsz, seq, n_a_heads, A_DV))
            qkvz3 = qkvz.reshape(bsz, seq, 4 * d)
            if paged is None:
                o = flash_diff_attention(qkvz3, wts['lam_a'][j], wts['subln_a'][j], lam_init, n_a_heads)
            else:
                ck, cv, pt = paged
                o = paged_diff_attention(qkvz3, ck, cv, pt, j, wts['lam_a'][j], wts['subln_a'][j], lam_init)
            o = o.reshape(t, d)
            z_src, z_col, w_out = qkvz, 3, wts['w_out_a'][j]
        else:
            hn = rms_norm_call(h, wts['norm_g'][i])
            hn3 = hn.reshape(bsz, seq, d)
            if wkv_init is None:
                s0 = jnp.zeros((bsz, d // B_HEAD, B_HEAD, B_HEAD), F32)
                sh0 = jnp.zeros((bsz, d), F32)
            else:
                s0, sh0 = wkv_init[j], shift_init[j]
            prev = jnp.concatenate([sh0[:, None, :], hn3[:, :-1]], axis=1).reshape(t, d)
            vmix = None
            if v_first is not None:
                vmix = (wts['v0_b'][j - 1], wts['v1_b'][j - 1], wts['v2_b'][j - 1], v_first)
            r, w, k, v, a, z = rwkv_projections(
                hn, prev, wts['mix_b'][j], wts['w_rkvz_b'][j], wts['w0_b'][j], wts['w1_b'][j], wts['w2_b'][j],
                wts['a0_b'][j], wts['a1_b'][j], wts['a2_b'][j], vmix)
            if v_first is None:
                v_first = v
            o, s_fin = wkv_scan(r, w, k, v, a, s0, wts['kk_b'][j], wts['ka_b'][j], wts['rk_b'][j],
                                wts['lnx_g_b'][j], wts['lnx_b_b'][j], bsz, seq)
            new_wkv.append(s_fin)
            new_shift.append(hn3[:, -1])
            z_src, z_col, w_out = z, 0, wts['w_out_b'][j]
        h = mixer_output(h, o, z_src, z_col, w_out, p[i].reshape(t, -1), wts['w_pe'][i], wts['w_pg'][i],
                         wts['norm_f'] if last else None)
    return (h.reshape(bsz, seq, d), jnp.stack(new_k), jnp.stack(new_v), jnp.stack(new_wkv),
            jnp.stack(new_shift))


_MATMUL_WEIGHTS = ('w_pe', 'w_pg', 'w_in_a', 'w_out_a', 'w_rkvz_b', 'w1_b', 'w2_b', 'a1_b', 'a2_b',
                   'v1_b', 'v2_b', 'w_out_b')


def kernel(x_prompt, x_sample, cache_k, cache_v, state_wkv, state_shift, page_table, p_prompt, p_sample, norm_g, norm_f, w_pe, w_pg, w_in_a, w_out_a, lam_a, subln_a, mix_b, w_rkvz_b, w0_b, w1_b, w2_b, a0_b, a1_b, a2_b, v0_b, v1_b, v2_b, kk_b, ka_b, rk_b, lnx_g_b, lnx_b_b, w_out_b):
    wts = dict(norm_g=norm_g, norm_f=norm_f, w_pe=w_pe, w_pg=w_pg, w_in_a=w_in_a,
               w_out_a=w_out_a, lam_a=lam_a, subln_a=subln_a, mix_b=mix_b, w_rkvz_b=w_rkvz_b,
               w0_b=w0_b, w1_b=w1_b, w2_b=w2_b, a0_b=a0_b, a1_b=a1_b, a2_b=a2_b,
               v0_b=v0_b, v1_b=v1_b, v2_b=v2_b, kk_b=kk_b, ka_b=ka_b, rk_b=rk_b,
               lnx_g_b=lnx_g_b, lnx_b_b=lnx_b_b, w_out_b=w_out_b)
    for name in _MATMUL_WEIGHTS:
        wts[name] = wts[name].astype(BF16)
    y_p, k_p, v_p, wkv_p, sh_p = _run_trunk(x_prompt, p_prompt, wts)
    y_s, k_s, v_s, wkv_s, sh_s = _run_trunk(x_sample, p_sample, wts, paged=(cache_k, cache_v, page_table),
                                            wkv_init=state_wkv, shift_init=state_shift)
    return (y_p, y_s, k_p, v_p, wkv_p, sh_p, k_s, v_s, wkv_s, sh_s)
```

```python
import functools
import math

import jax
import jax.numpy as jnp
from jax import lax
from jax.experimental import pallas as pl
from jax.experimental.pallas import tpu as pltpu

F32 = jnp.float32
BF16 = jnp.bfloat16

LOG2_E = math.log2(math.e)
NORM_EPS = 1e-6
SUBLN_EPS = 1e-5
LNX_EPS = 64e-5
N_MIXERS = 2

A_DK = 64
A_DV = 2 * A_DK
B_HEAD = 64
LANES = 128
SCAN_HEADS = LANES // 2
SCAN_ROWS = B_HEAD // 2
VMEM_LIMIT = 48 * 1024 * 1024
CT_PITCH = B_HEAD + 8


def _cparams(*sem):
    return pltpu.CompilerParams(dimension_semantics=sem, vmem_limit_bytes=VMEM_LIMIT)


def _row_tile(t, cap):
    tile = min(t, cap)
    assert t % tile == 0, (t, tile)
    return tile


def _rms(x, eps):
    return x * lax.rsqrt(jnp.mean(x * x, axis=-1, keepdims=True) + eps)


def _sigmoid(x):
    return 1.0 / (1.0 + jnp.exp(-x))


def _dot(a, b):
    return jnp.dot(a, b, preferred_element_type=F32)


def _dot_nt(a, b):
    return lax.dot_general(a, b, (((1,), (1,)), ((), ())), preferred_element_type=F32)


def _norm_kernel(x_ref, g_ref, o_ref):
    o_ref[...] = _rms(x_ref[...], NORM_EPS) * g_ref[...]


def rms_norm_call(x, g):
    t, d = x.shape
    tm = _row_tile(t, 512)
    return pl.pallas_call(
        _norm_kernel,
        grid=(t // tm,),
        in_specs=[pl.BlockSpec((tm, d), lambda i: (i, 0)), pl.BlockSpec((1, d), lambda i: (0, 0))],
        out_specs=pl.BlockSpec((tm, d), lambda i: (i, 0)),
        out_shape=jax.ShapeDtypeStruct((t, d), F32),
        compiler_params=_cparams("parallel"),
        name="rms_norm",
    )(x, g.reshape(1, d))


def _norm_mm_kernel(x_ref, g_ref, w_ref, o_ref, xn_ref):
    @pl.when(pl.program_id(1) == 0)
    def _():
        xn_ref[...] = (_rms(x_ref[...], NORM_EPS) * g_ref[...]).astype(BF16)

    o_ref[...] = _dot(xn_ref[...], w_ref[...])


def norm_matmul(x, g, w):
    t, d = x.shape
    n = w.shape[1]
    tm = _row_tile(t, 1024)
    tn = _row_tile(n, 1024)
    return pl.pallas_call(
        _norm_mm_kernel,
        grid=(t // tm, n // tn),
        in_specs=[pl.BlockSpec((tm, d), lambda i, j: (i, 0)),
                  pl.BlockSpec((1, d), lambda i, j: (0, 0)),
                  pl.BlockSpec((d, tn), lambda i, j: (0, j))],
        out_specs=pl.BlockSpec((tm, tn), lambda i, j: (i, j)),
        out_shape=jax.ShapeDtypeStruct((t, n), F32),
        scratch_shapes=[pltpu.VMEM((tm, d), BF16)],
        compiler_params=_cparams("parallel", "arbitrary"),
        name="norm_matmul",
    )(x, g.reshape(1, d), w)


def _lam(lam_ref, lam_init):
    lp = lam_ref[...]
    s1 = jnp.sum(lp[0:1] * lp[1:2], axis=-1, keepdims=True)
    s2 = jnp.sum(lp[2:3] * lp[3:4], axis=-1, keepdims=True)
    return jnp.exp(s1) - jnp.exp(s2) + lam_init


def _flash_kernel(lam_ref, g_ref, q_ref, k_ref, v_ref, o_ref, kb_ref, vt_ref, m_ref, l_ref, acc_ref,
                  sa_ref, sb_ref, *, tq, lam_init):
    qi = pl.program_id(2)
    seq = k_ref.shape[1]

    @pl.when(qi == 0)
    def _():
        kb_ref[...] = k_ref[0].astype(BF16)
        for c in range(seq // LANES):
            sl = slice(c * LANES, (c + 1) * LANES)
            vt_ref[:, sl] = v_ref[0, sl, :].T.astype(BF16)

    qt = q_ref[0].T * (A_DK ** -0.5 * LOG2_E)
    chan = lax.broadcasted_iota(jnp.int32, qt.shape, 0)
    qt2 = jnp.concatenate([jnp.where(chan < A_DK, qt, 0.0), jnp.where(chan >= A_DK, qt, 0.0)],
                          axis=1).astype(BF16)
    m_ref[...] = jnp.full(m_ref.shape, -jnp.inf, F32)
    l_ref[...] = jnp.zeros(l_ref.shape, F32)
    acc_ref[...] = jnp.zeros(acc_ref.shape, F32)

    def scores(kb):
        start = pl.multiple_of(kb * tq, tq)
        return _dot(kb_ref[pl.ds(start, tq), :], qt2)

    def update(kb, s):
        start = pl.multiple_of(kb * tq, tq)
        m_prev = m_ref[...]
        m_new = jnp.maximum(m_prev, jnp.max(s, axis=0, keepdims=True))
        alpha = jnp.exp2(m_prev - m_new)
        p = jnp.exp2(s - m_new)
        l_ref[...] = alpha * l_ref[...] + jnp.sum(p, axis=0, keepdims=True)
        acc_ref[...] = alpha * acc_ref[...] + _dot(vt_ref[:, pl.ds(start, tq)], p.astype(BF16))
        m_ref[...] = m_new

    def diagonal(s_buf):
        s = s_buf[...]
        key = lax.broadcasted_iota(jnp.int32, s.shape, 0)
        qry = lax.broadcasted_iota(jnp.int32, s.shape, 1)
        qry = jnp.where(qry >= tq, qry - tq, qry)
        update(qi, jnp.where(key <= qry, s, -jnp.inf))

    sa_ref[...] = scores(0)

    def pair(i, carry):
        kb = 2 * i
        sb_ref[...] = scores(kb + 1)
        update(kb, sa_ref[...])
        sa_ref[...] = scores(kb + 2)
        update(kb + 1, sb_ref[...])
        return carry

    lax.fori_loop(0, qi // 2, pair, 0)

    @pl.when(qi % 2 == 0)
    def _():
        diagonal(sa_ref)

    @pl.when(qi % 2 == 1)
    def _():
        sb_ref[...] = scores(qi)
        update(qi - 1, sa_ref[...])
        diagonal(sb_ref)

    lam = _lam(lam_ref, lam_init)
    o = acc_ref[:, 0:tq] / l_ref[:, 0:tq] - lam * (acc_ref[:, tq:2 * tq] / l_ref[:, tq:2 * tq])
    o = o * lax.rsqrt(jnp.mean(o * o, axis=0, keepdims=True) + SUBLN_EPS)
    o_ref[0] = (o * (g_ref[...] * (1.0 - lam_init))).T


def flash_diff_attention(qkvz, lam_p, subln, lam_init, n_heads, tq_cap=512):
    b, s, _ = qkvz.shape
    tq = _row_tile(s, tq_cap)
    assert s % LANES == 0 and tq % LANES == 0
    kern = functools.partial(_flash_kernel, tq=tq, lam_init=lam_init)
    return pl.pallas_call(
        kern,
        grid=(b, n_heads, s // tq),
        in_specs=[pl.BlockSpec((4, A_DK), lambda bi, h, qi: (0, 0)),
                  pl.BlockSpec((A_DV, 1), lambda bi, h, qi: (0, 0)),
                  pl.BlockSpec((1, tq, A_DV), lambda bi, h, qi: (bi, qi, h)),
                  pl.BlockSpec((1, s, A_DV), lambda bi, h, qi: (bi, 0, n_heads + h)),
                  pl.BlockSpec((1, s, A_DV), lambda bi, h, qi: (bi, 0, 2 * n_heads + h))],
        out_specs=pl.BlockSpec((1, tq, A_DV), lambda bi, h, qi: (bi, qi, h)),
        out_shape=jax.ShapeDtypeStruct((b, s, n_heads * A_DV), F32),
        scratch_shapes=[pltpu.VMEM((s, A_DV), BF16), pltpu.VMEM((A_DV, s), BF16),
                        pltpu.VMEM((1, 2 * tq), F32), pltpu.VMEM((1, 2 * tq), F32),
                        pltpu.VMEM((A_DV, 2 * tq), F32), pltpu.VMEM((tq, 2 * tq), F32),
                        pltpu.VMEM((tq, 2 * tq), F32)],
        compiler_params=_cparams("parallel", "parallel", "arbitrary"),
        name="flash_diff_attention",
    )(lam_p, subln.reshape(A_DV, 1), qkvz, qkvz, qkvz)


def _paged_kernel(pt_ref, lam_ref, g_ref, x_ref, *rest, pages, n_steps, width, lam_init):
    kt_refs = rest[:pages]
    v_refs = rest[pages:2 * pages]
    o_ref, qm_ref, m_ref, l_ref, acc_ref, fin_ref = rest[2 * pages:]
    step = pl.program_id(1)
    heads = width // A_DV
    rows = 2 * heads
    page = kt_refs[0].shape[1]
    head_of_row = lax.broadcasted_iota(jnp.int32, (rows, A_DV), 0) // 2

    @pl.when(step == 0)
    def _():
        q = x_ref[0, :, 0:width] * (A_DK ** -0.5)
        r = lax.broadcasted_iota(jnp.int32, (rows, width), 0)
        lane = lax.broadcasted_iota(jnp.int32, (rows, width), 1)
        qm_ref[...] = jnp.where(lane // A_DK == r, q, 0.0)
        m_ref[...] = jnp.full(m_ref.shape, -jnp.inf, F32)
        l_ref[...] = jnp.zeros(l_ref.shape, F32)
        acc_ref[...] = jnp.zeros(acc_ref.shape, F32)

    qm = qm_ref[...].astype(BF16)
    s = jnp.concatenate([_dot(qm, kt_refs[i][...].astype(BF16)) for i in range(pages)], axis=-1)
    m_prev = m_ref[...]
    m_new = jnp.maximum(m_prev, jnp.max(s, axis=-1, keepdims=True))
    alpha = jnp.exp(m_prev - m_new)
    p = jnp.exp(s - m_new)
    l_ref[...] = alpha * l_ref[...] + jnp.sum(p, axis=-1, keepdims=True)
    p = p.astype(BF16)
    pv = jnp.zeros((rows, A_DV), F32)
    for i in range(pages):
        pi = p[:, i * page:(i + 1) * page]
        for h in range(heads):
            vh = v_refs[i][pl.ds(h, page, stride=heads), :].astype(BF16)
            pv = pv + jnp.where(head_of_row == h, _dot(pi, vh), 0.0)
    acc_ref[...] = alpha * acc_ref[...] + pv
    m_ref[...] = m_new

    @pl.when(step == n_steps - 1)
    def _():
        k_cur = x_ref[0, :, width:2 * width]
        s_cur = jnp.sum(qm_ref[...] * k_cur, axis=-1, keepdims=True)
        v_cur = jnp.zeros((rows, A_DV), F32)
        for h in range(heads):
            vh = x_ref[0, :, 2 * width + h * A_DV:2 * width + (h + 1) * A_DV]
            v_cur = jnp.where(head_of_row == h, vh, v_cur)
        m_fin = jnp.maximum(m_ref[...], s_cur)
        a_fin = jnp.exp(m_ref[...] - m_fin)
        p_cur = jnp.exp(s_cur - m_fin)
        l_fin = a_fin * l_ref[...] + p_cur
        fin_ref[...] = (a_fin * acc_ref[...] + p_cur * v_cur) / l_fin
        o = fin_ref[pl.ds(0, heads, stride=2), :] - _lam(lam_ref, lam_init) * fin_ref[pl.ds(1, heads, stride=2), :]
        o_ref[0] = _rms(o, SUBLN_EPS) * g_ref[...] * (1.0 - lam_init)


def paged_diff_attention(qkvz, cache_k, cache_v, page_table, layer, lam_p, subln, lam_init, pages=8):
    bsz, _, four_w = qkvz.shape
    width = four_w // 4
    heads = width // A_DV
    n_pages = page_table.shape[1]
    pages = math.gcd(pages, n_pages)
    n_steps = n_pages // pages
    n_layers, n_phys, page_size = cache_k.shape[:3]
    ckt = jnp.transpose(cache_k, (0, 1, 3, 4, 5, 2)).reshape(n_layers, n_phys, width, page_size)
    cv = cache_v.reshape(n_layers, n_phys, page_size * heads, A_DV)
    rows = 2 * heads

    def page_spec(i, shape):
        return pl.BlockSpec((None, None) + shape, lambda b, p, pt: (layer, pt[b, p * pages + i], 0, 0))

    kern = functools.partial(_paged_kernel, pages=pages, n_steps=n_steps, width=width, lam_init=lam_init)
    grid_spec = pltpu.PrefetchScalarGridSpec(
        num_scalar_prefetch=1,
        grid=(bsz, n_steps),
        in_specs=[pl.BlockSpec((4, A_DK), lambda b, p, pt: (0, 0)),
                  pl.BlockSpec((1, A_DV), lambda b, p, pt: (0, 0)),
                  pl.BlockSpec((1, 1, four_w), lambda b, p, pt: (b, 0, 0))]
                 + [page_spec(i, (width, page_size)) for i in range(pages)]
                 + [page_spec(i, (page_size * heads, A_DV)) for i in range(pages)],
        out_specs=pl.BlockSpec((1, heads, A_DV), lambda b, p, pt: (b, 0, 0)),
        scratch_shapes=[pltpu.VMEM((rows, width), F32), pltpu.VMEM((rows, 1), F32),
                        pltpu.VMEM((rows, 1), F32), pltpu.VMEM((rows, A_DV), F32),
                        pltpu.VMEM((rows, A_DV), F32)])
    return pl.pallas_call(
        kern,
        grid_spec=grid_spec,
        out_shape=jax.ShapeDtypeStruct((bsz, heads, A_DV), F32),
        compiler_params=_cparams("parallel", "arbitrary"),
        name="paged_diff_attention",
    )(page_table, lam_p, subln.reshape(1, A_DV), qkvz, *([ckt] * pages), *([cv] * pages))


def _rwkv_pre_kernel(*refs, has_vmix):
    (hn_ref, prev_ref, mix_ref, wrkvz_ref, w0_ref, w1_ref, w2_ref, a0_ref, a1_ref, a2_ref) = refs[:10]
    refs = refs[10:]
    if has_vmix:
        v0_ref, v1_ref, v2_ref, vf_ref = refs[:4]
        refs = refs[4:]
    r_ref, w_ref, k_ref, v_ref, a_ref, z_ref = refs

    hn = hn_ref[...]
    delta = prev_ref[...] - hn

    def mixed(m):
        return (hn + delta * mix_ref[m:m + 1, :]).astype(BF16)

    xv = mixed(2)
    r_ref[...] = _dot(mixed(0), wrkvz_ref[0])
    k_ref[...] = _dot(mixed(1), wrkvz_ref[1])
    v = _dot(xv, wrkvz_ref[2])
    z_ref[...] = _dot(mixed(3), wrkvz_ref[3])
    lw = _dot(jnp.tanh(_dot(mixed(4), w1_ref[...])).astype(BF16), w2_ref[...])
    x = w0_ref[...] + lw
    w_ref[...] = -(jnp.maximum(-x, 0.0) + jnp.log1p(jnp.exp(-jnp.abs(x)))) - 0.5
    la = _dot(_dot(mixed(5), a1_ref[...]).astype(BF16), a2_ref[...])
    a_ref[...] = _sigmoid(a0_ref[...] + la)
    if has_vmix:
        lv = _dot(_dot(xv, v1_ref[...]).astype(BF16), v2_ref[...])
        v = v + (vf_ref[...] - v) * _sigmoid(v0_ref[...] + lv)
    v_ref[...] = v


def rwkv_projections(hn, prev, mix, wrkvz, w0, w1, w2, a0, a1, a2, vmix=None):
    t, d = hn.shape
    tm = _row_tile(t, 256)
    row = pl.BlockSpec((tm, d), lambda i: (i, 0))

    def whole(x):
        return pl.BlockSpec(x.shape, lambda i: (0,) * x.ndim)

    args = [hn, prev, mix, wrkvz, w0.reshape(1, d), w1, w2, a0.reshape(1, d), a1, a2]
    in_specs = [row, row] + [whole(x) for x in args[2:]]
    if vmix is not None:
        v0, v1, v2, v_first = vmix
        extra = [v0.reshape(1, d), v1, v2]
        args += extra + [v_first]
        in_specs += [whole(x) for x in extra] + [row]
    return pl.pallas_call(
        functools.partial(_rwkv_pre_kernel, has_vmix=vmix is not None),
        grid=(t // tm,),
        in_specs=in_specs,
        out_specs=[row] * 6,
        out_shape=[jax.ShapeDtypeStruct((t, d), F32)] * 6,
        compiler_params=_cparams("parallel"),
        name="rwkv_projections",
    )(*args)


def _scan_kernel(r_ref, w_ref, k_ref, a_ref, v_ref, s0_ref, kkb_ref, ka_ref, rk_ref, lng_ref, lnb_ref,
                 o_ref, sfin_ref, st_ref, ob_ref, *, tb, n_tb, pitch):
    tblk = pl.program_id(1)

    @pl.when(tblk == 0)
    def _():
        st_ref[...] = s0_ref[0]

    def step(t, carry):
        key_rows = pl.ds(t, B_HEAD, stride=pitch)
        val_rows = pl.ds(t, SCAN_ROWS, stride=pitch)
        r = r_ref[key_rows, :]
        k = k_ref[key_rows, :]
        a = a_ref[key_rows, :]
        decay = jnp.exp(-jnp.exp(w_ref[key_rows, :]))
        kk = k * kkb_ref[...]
        kk = kk / jnp.maximum(jnp.sqrt(jnp.sum(kk * kk, axis=0, keepdims=True)), 1e-12)
        k = k * (1.0 + (a - 1.0) * ka_ref[...])
        neg_kk = -kk
        kka = kk * a
        dr = decay * r
        kka_r = jnp.sum(kka * r, axis=0, keepdims=True)
        k_r = jnp.sum(k * r, axis=0, keepdims=True)
        for i in range(SCAN_ROWS):
            s = st_ref[i]
            sa = jnp.sum(s * neg_kk, axis=0, keepdims=True)
            so = jnp.sum(s * dr, axis=0, keepdims=True)
            vi = v_ref[pl.ds(i * pitch + t, 1), :]
            st_ref[i] = s * decay + sa * kka + vi * k
            ob_ref[pl.ds(i, 1), :] = so + sa * kka_r + vi * k_r
        o = ob_ref[...]

        def head_mean(x):
            tot = jnp.sum(x, axis=0, keepdims=True)
            return (tot + pltpu.roll(tot, SCAN_HEADS, axis=1)) * (1.0 / B_HEAD)

        dev = o - head_mean(o)
        o = dev * lax.rsqrt(head_mean(dev * dev) + LNX_EPS) * lng_ref[...] + lnb_ref[...]
        bonus = jnp.sum(r * k * rk_ref[...], axis=0, keepdims=True)
        o_ref[val_rows, :] = o + bonus * v_ref[val_rows, :]
        return carry

    lax.fori_loop(0, tb, step, 0, unroll=4 if tb % 4 == 0 else 1)

    @pl.when(tblk == n_tb - 1)
    def _():
        sfin_ref[0] = st_ref[...]


def _to_scan_kernel(x_ref, o_ref, ct_ref, *, rows, second, tb, pitch):
    bl, steps, d = x_ref.shape
    heads = d // B_HEAD
    pair = LANES // B_HEAD
    for b in range(bl):
        for c in range(d // LANES):
            xt = x_ref[b, :, c * LANES:(c + 1) * LANES].T
            for h in range(pair):
                ct_ref[pl.ds((b * heads + c * pair + h) * CT_PITCH, B_HEAD), :] = xt[h * B_HEAD:(h + 1) * B_HEAD]
    for r in range(rows):
        lo = ct_ref[pl.ds(r, SCAN_HEADS, stride=CT_PITCH), :]
        hi = ct_ref[pl.ds(second + r, SCAN_HEADS, stride=CT_PITCH), :]
        xt = jnp.concatenate([lo, hi], axis=0).T
        for q in range(steps // tb):
            o_ref[pl.ds((q * rows + r) * pitch, tb), :] = xt[q * tb:(q + 1) * tb]


def _from_scan_kernel(x_ref, o_ref, ct_ref, *, tb, pitch):
    bl, steps, d = o_ref.shape
    heads = d // B_HEAD
    for r in range(SCAN_ROWS):
        x = jnp.concatenate([x_ref[pl.ds((q * SCAN_ROWS + r) * pitch, tb), :] for q in range(steps // tb)], axis=0)
        xt = x.T
        for half in range(2):
            ct_ref[pl.ds((half * SCAN_ROWS + r) * CT_PITCH, SCAN_HEADS), :] = xt[half * SCAN_HEADS:(half + 1) * SCAN_HEADS]
    for b in range(bl):
        for c in range(d // LANES):
            first = b * heads + c * (LANES // B_HEAD)
            tile = jnp.concatenate([ct_ref[pl.ds(first + h, B_HEAD, stride=CT_PITCH), :]
                                    for h in range(LANES // B_HEAD)], axis=0)
            o_ref[b, :, c * LANES:(c + 1) * LANES] = tile.T


def to_scan_layout(x, bsz, seq, rows, second, tb):
    d = x.shape[1]
    pitch = _scan_pitch(tb)
    blk_rows = (LANES // tb) * rows * pitch
    out = pl.pallas_call(
        functools.partial(_to_scan_kernel, rows=rows, second=second, tb=tb, pitch=pitch),
        grid=(seq // LANES,),
        in_specs=[pl.BlockSpec((bsz, LANES, d), lambda t: (0, t, 0))],
        out_specs=pl.BlockSpec((blk_rows, LANES), lambda t: (t, 0)),
        out_shape=jax.ShapeDtypeStruct((seq // LANES * blk_rows, LANES), F32),
        scratch_shapes=[pltpu.VMEM((SCAN_HEADS * CT_PITCH, LANES), F32)],
        compiler_params=_cparams("parallel"),
        name="to_scan_layout",
    )(x.reshape(bsz, seq, d))
    return out.reshape(1, seq // tb, rows * pitch, LANES)


def from_scan_layout(o, bsz, seq, d, tb):
    pitch = _scan_pitch(tb)
    blk_rows = (LANES // tb) * SCAN_ROWS * pitch
    out = pl.pallas_call(
        functools.partial(_from_scan_kernel, tb=tb, pitch=pitch),
        grid=(seq // LANES,),
        in_specs=[pl.BlockSpec((blk_rows, LANES), lambda t: (t, 0))],
        out_specs=pl.BlockSpec((bsz, LANES, d), lambda t: (0, t, 0)),
        out_shape=jax.ShapeDtypeStruct((bsz, seq, d), F32),
        scratch_shapes=[pltpu.VMEM((B_HEAD * CT_PITCH, LANES), F32)],
        compiler_params=_cparams("parallel"),
        name="from_scan_layout",
    )(o.reshape(seq // LANES * blk_rows, LANES))
    return out.reshape(bsz * seq, d)


def _scan_pitch(tb):
    return tb if tb == 1 else tb + 8


def _scan_lane_param(p, heads):
    return jnp.tile(p.reshape(heads, B_HEAD).T, (1, LANES // heads))


def _scan_row_param(p, heads):
    x = p.reshape(heads, 2, SCAN_ROWS).transpose(2, 1, 0)
    x = jnp.broadcast_to(x[:, :, None, :], (SCAN_ROWS, 2, SCAN_HEADS // heads, heads))
    return x.reshape(SCAN_ROWS, LANES)


def wkv_scan(r, w, k, v, a, s0, kk_b, ka_b, rk_b, lnx_g, lnx_b, bsz, seq, tb_cap=32):
    d = r.shape[1]
    heads = d // B_HEAD
    bl = SCAN_HEADS // heads
    assert SCAN_HEADS % heads == 0 and bsz % bl == 0
    nblk = bsz // bl
    tb = _row_tile(seq, tb_cap)
    n_tb = seq // tb
    pitch = _scan_pitch(tb)

    def blocked(x, rows):
        x = x.reshape(nblk, n_tb, tb, rows, LANES).transpose(0, 1, 3, 2, 4)
        x = jnp.pad(x, ((0, 0), (0, 0), (0, 0), (0, pitch - tb), (0, 0)))
        return x.reshape(nblk, n_tb, rows * pitch, LANES)

    def key_layout(x):
        x = x.reshape(nblk, bl, seq, heads, B_HEAD).transpose(0, 2, 4, 1, 3)
        x = x.reshape(nblk, seq, B_HEAD, SCAN_HEADS)
        return blocked(jnp.concatenate([x, x], axis=-1), B_HEAD)

    def val_layout(x):
        x = x.reshape(nblk, bl, seq, heads, 2, SCAN_ROWS).transpose(0, 2, 5, 4, 1, 3)
        return blocked(x.reshape(nblk, seq, SCAN_ROWS, LANES), SCAN_ROWS)

    in_kernel_layout = nblk == 1 and seq % LANES == 0
    if in_kernel_layout:
        key_layout = functools.partial(to_scan_layout, bsz=bsz, seq=seq, rows=B_HEAD, second=0, tb=tb)
        val_layout = functools.partial(to_scan_layout, bsz=bsz, seq=seq, rows=SCAN_ROWS, second=SCAN_ROWS, tb=tb)

    st0 = s0.astype(F32).reshape(nblk, bl, heads, 2, SCAN_ROWS, B_HEAD).transpose(0, 4, 5, 3, 1, 2)
    st0 = st0.reshape(nblk, SCAN_ROWS, B_HEAD, LANES)

    key_spec = pl.BlockSpec((None, None, B_HEAD * pitch, LANES), lambda n, t: (n, t, 0, 0))
    val_spec = pl.BlockSpec((None, None, SCAN_ROWS * pitch, LANES), lambda n, t: (n, t, 0, 0))
    st_spec = pl.BlockSpec((1, SCAN_ROWS, B_HEAD, LANES), lambda n, t: (n, 0, 0, 0))
    kp_spec = pl.BlockSpec((B_HEAD, LANES), lambda n, t: (0, 0))
    vp_spec = pl.BlockSpec((SCAN_ROWS, LANES), lambda n, t: (0, 0))
    o, sfin = pl.pallas_call(
        functools.partial(_scan_kernel, tb=tb, n_tb=n_tb, pitch=pitch),
        grid=(nblk, n_tb),
        in_specs=[key_spec] * 4 + [val_spec, st_spec] + [kp_spec] * 3 + [vp_spec] * 2,
        out_specs=[val_spec, st_spec],
        out_shape=[jax.ShapeDtypeStruct((nblk, n_tb, SCAN_ROWS * pitch, LANES), F32),
                   jax.ShapeDtypeStruct((nblk, SCAN_ROWS, B_HEAD, LANES), F32)],
        scratch_shapes=[pltpu.VMEM((SCAN_ROWS, B_HEAD, LANES), F32), pltpu.VMEM((SCAN_ROWS, LANES), F32)],
        compiler_params=_cparams("parallel", "arbitrary"),
        name="wkv_scan",
    )(key_layout(r), key_layout(w), key_layout(k), key_layout(a), val_layout(v), st0,
      _scan_lane_param(kk_b, heads), _scan_lane_param(ka_b, heads), _scan_lane_param(rk_b.reshape(-1), heads),
      _scan_row_param(lnx_g, heads), _scan_row_param(lnx_b, heads))
    if in_kernel_layout:
        o = from_scan_layout(o, bsz, seq, d, tb)
    else:
        o = o.reshape(nblk, n_tb, SCAN_ROWS, pitch, 2, bl, heads)[:, :, :, :tb]
        o = o.transpose(0, 5, 1, 3, 6, 4, 2).reshape(bsz * seq, d)
    sfin = sfin.reshape(nblk, SCAN_ROWS, B_HEAD, 2, bl, heads).transpose(0, 4, 5, 3, 1, 2)
    return o, sfin.reshape(bsz, heads, B_HEAD, B_HEAD)


def _post_kernel(*refs, final):
    h_ref, o_ref, z_ref, wout_ref, p_ref, wpe_ref, wpg_ref = refs[:7]
    refs = refs[7:]
    if final:
        nf_ref, out_ref = refs
    else:
        (out_ref,) = refs
    z = z_ref[...]
    x = (o_ref[...] * (z * _sigmoid(z))).astype(BF16)
    h = h_ref[...] + _dot(x, wout_ref[...])
    gate = _sigmoid(_dot(_rms(h, NORM_EPS).astype(BF16), wpg_ref[...]))
    h = h + _dot(p_ref[...].astype(BF16), wpe_ref[...]) * gate
    if final:
        h = _rms(h, NORM_EPS) * nf_ref[...]
    out_ref[...] = h


def mixer_output(h, o, z_src, z_col, w_out, p, w_pe, w_pg, norm_f=None):
    t, d = h.shape
    tm = _row_tile(t, 512)
    row = pl.BlockSpec((tm, d), lambda i: (i, 0))

    def whole(x):
        return pl.BlockSpec(x.shape, lambda i: (0,) * x.ndim)

    args = [h, o, z_src, w_out, p, w_pe, w_pg]
    in_specs = [row, row, pl.BlockSpec((tm, d), lambda i: (i, z_col)), whole(w_out),
                pl.BlockSpec((tm, p.shape[1]), lambda i: (i, 0)), whole(w_pe), whole(w_pg)]
    if norm_f is not None:
        args.append(norm_f.reshape(1, d))
        in_specs.append(whole(args[-1]))
    return pl.pallas_call(
        functools.partial(_post_kernel, final=norm_f is not None),
        grid=(t // tm,),
        in_specs=in_specs,
        out_specs=row,
        out_shape=jax.ShapeDtypeStruct((t, d), F32),
        compiler_params=_cparams("parallel"),
        name="mixer_output",
    )(*args)


def _run_trunk(x, p, wts, paged=None, wkv_init=None, shift_init=None):
    bsz, seq, d = x.shape
    t = bsz * seq
    depth = p.shape[0]
    n_a_heads = d // A_DV
    h = x.reshape(t, d)
    v_first = None
    new_k, new_v, new_wkv, new_shift = [], [], [], []
    for i in range(depth):
        j = i // N_MIXERS
        last = i == depth - 1
        if i % N_MIXERS == 0:
            lam_init = 0.8 - 0.6 * math.exp(-0.3 * i)
            qkvz = norm_matmul(h, wts['norm_g'][i], wts['w_in_a'][j])
            new_k.append(qkvz[:, d:2 * d].reshape(bsz, seq, n_a_heads, 2, A_DK))
            new_v.append(qkvz[:, 2 * d:3 * d].reshape(bsz, seq, n_a_heads, A_DV))
            qkvz3 = qkvz.reshape(bsz, seq, 4 * d)
            if paged is None:
                o = flash_diff_attention(qkvz3, wts['lam_a'][j], wts['subln_a'][j], lam_init, n_a_heads)
            else:
                ck, cv, pt = paged
                o = paged_diff_attention(qkvz3, ck, cv, pt, j, wts['lam_a'][j], wts['subln_a'][j], lam_init)
            o = o.reshape(t, d)
            z_src, z_col, w_out = qkvz, 3, wts['w_out_a'][j]
        else:
            hn = rms_norm_call(h, wts['norm_g'][i])
            hn3 = hn.reshape(bsz, seq, d)
            if wkv_init is None:
                s0 = jnp.zeros((bsz, d // B_HEAD, B_HEAD, B_HEAD), F32)
                sh0 = jnp.zeros((bsz, d), F32)
            else:
                s0, sh0 = wkv_init[j], shift_init[j]
            prev = jnp.concatenate([sh0[:, None, :], hn3[:, :-1]], axis=1).reshape(t, d)
            vmix = None
            if v_first is not None:
                vmix = (wts['v0_b'][j - 1], wts['v1_b'][j - 1], wts['v2_b'][j - 1], v_first)
            r, w, k, v, a, z = rwkv_projections(
                hn, prev, wts['mix_b'][j], wts['w_rkvz_b'][j], wts['w0_b'][j], wts['w1_b'][j], wts['w2_b'][j],
                wts['a0_b'][j], wts['a1_b'][j], wts['a2_b'][j], vmix)
            if v_first is None:
                v_first = v
            o, s_fin = wkv_scan(r, w, k, v, a, s0, wts['kk_b'][j], wts['ka_b'][j], wts['rk_b'][j],
                                wts['lnx_g_b'][j], wts['lnx_b_b'][j], bsz, seq)
            new_wkv.append(s_fin)
            new_shift.append(hn3[:, -1])
            z_src, z_col, w_out = z, 0, wts['w_out_b'][j]
        h = mixer_output(h, o, z_src, z_col, w_out, p[i].reshape(t, -1), wts['w_pe'][i], wts['w_pg'][i],
                         wts['norm_f'] if last else None)
    return (h.reshape(bsz, seq, d), jnp.stack(new_k), jnp.stack(new_v), jnp.stack(new_wkv),
            jnp.stack(new_shift))


_MATMUL_WEIGHTS = ('w_pe', 'w_pg', 'w_in_a', 'w_out_a', 'w_rkvz_b', 'w1_b', 'w2_b', 'a1_b', 'a2_b',
                   'v1_b', 'v2_b', 'w_out_b')


def kernel(x_prompt, x_sample, cache_k, cache_v, state_wkv, state_shift, page_table, p_prompt, p_sample, norm_g, norm_f, w_pe, w_pg, w_in_a, w_out_a, lam_a, subln_a, mix_b, w_rkvz_b, w0_b, w1_b, w2_b, a0_b, a1_b, a2_b, v0_b, v1_b, v2_b, kk_b, ka_b, rk_b, lnx_g_b, lnx_b_b, w_out_b):
    wts = dict(norm_g=norm_g, norm_f=norm_f, w_pe=w_pe, w_pg=w_pg, w_in_a=w_in_a,
               w_out_a=w_out_a, lam_a=lam_a, subln_a=subln_a, mix_b=mix_b, w_rkvz_b=w_rkvz_b,
               w0_b=w0_b, w1_b=w1_b, w2_b=w2_b, a0_b=a0_b, a1_b=a1_b, a2_b=a2_b,
               v0_b=v0_b, v1_b=v1_b, v2_b=v2_b, kk_b=kk_b, ka_b=ka_b, rk_b=rk_b,
               lnx_g_b=lnx_g_b, lnx_b_b=lnx_b_b, w_out_b=w_out_b)
    for name in _MATMUL_WEIGHTS:
        wts[name] = wts[name].astype(BF16)
    y_p, k_p, v_p, wkv_p, sh_p = _run_trunk(x_prompt, p_prompt, wts)
    y_s, k_s, v_s, wkv_s, sh_s = _run_trunk(x_sample, p_sample, wts, paged=(cache_k, cache_v, page_table),
                                            wkv_init=state_wkv, shift_init=state_shift)
    return (y_p, y_s, k_p, v_p, wkv_p, sh_p, k_s, v_s, wkv_s, sh_s)
```

```python
import functools
import math

import jax
import jax.numpy as jnp
from jax import lax
from jax.experimental import pallas as pl
from jax.experimental.pallas import tpu as pltpu

F32 = jnp.float32
BF16 = jnp.bfloat16

LOG2_E = math.log2(math.e)
NORM_EPS = 1e-6
SUBLN_EPS = 1e-5
LNX_EPS = 64e-5
N_MIXERS = 2

A_DK = 64
A_DV = 2 * A_DK
B_HEAD = 64
LANES = 128
SCAN_HEADS = LANES // 2
SCAN_ROWS = B_HEAD // 2
VMEM_LIMIT = 48 * 1024 * 1024
CT_PITCH = B_HEAD + 8


def _cparams(*sem):
    return pltpu.CompilerParams(dimension_semantics=sem, vmem_limit_bytes=VMEM_LIMIT)


def _row_tile(t, cap):
    tile = min(t, cap)
    assert t % tile == 0, (t, tile)
    return tile


def _rms(x, eps):
    return x * lax.rsqrt(jnp.mean(x * x, axis=-1, keepdims=True) + eps)


def _sigmoid(x):
    return 1.0 / (1.0 + jnp.exp(-x))


def _dot(a, b):
    return jnp.dot(a, b, preferred_element_type=F32)


def _dot_nt(a, b):
    return lax.dot_general(a, b, (((1,), (1,)), ((), ())), preferred_element_type=F32)


def _norm_kernel(x_ref, g_ref, o_ref):
    o_ref[...] = _rms(x_ref[...], NORM_EPS) * g_ref[...]


def rms_norm_call(x, g):
    t, d = x.shape
    tm = _row_tile(t, 512)
    return pl.pallas_call(
        _norm_kernel,
        grid=(t // tm,),
        in_specs=[pl.BlockSpec((tm, d), lambda i: (i, 0)), pl.BlockSpec((1, d), lambda i: (0, 0))],
        out_specs=pl.BlockSpec((tm, d), lambda i: (i, 0)),
        out_shape=jax.ShapeDtypeStruct((t, d), F32),
        compiler_params=_cparams("parallel"),
        name="rms_norm",
    )(x, g.reshape(1, d))


def _norm_mm_kernel(x_ref, g_ref, w_ref, o_ref, xn_ref):
    @pl.when(pl.program_id(1) == 0)
    def _():
        xn_ref[...] = (_rms(x_ref[...], NORM_EPS) * g_ref[...]).astype(BF16)

    o_ref[...] = _dot(xn_ref[...], w_ref[...])


def norm_matmul(x, g, w):
    t, d = x.shape
    n = w.shape[1]
    tm = _row_tile(t, 1024)
    tn = _row_tile(n, 1024)
    return pl.pallas_call(
        _norm_mm_kernel,
        grid=(t // tm, n // tn),
        in_specs=[pl.BlockSpec((tm, d), lambda i, j: (i, 0)),
                  pl.BlockSpec((1, d), lambda i, j: (0, 0)),
                  pl.BlockSpec((d, tn), lambda i, j: (0, j))],
        out_specs=pl.BlockSpec((tm, tn), lambda i, j: (i, j)),
        out_shape=jax.ShapeDtypeStruct((t, n), F32),
        scratch_shapes=[pltpu.VMEM((tm, d), BF16)],
        compiler_params=_cparams("parallel", "arbitrary"),
        name="norm_matmul",
    )(x, g.reshape(1, d), w)


def _lam(lam_ref, lam_init):
    lp = lam_ref[...]
    s1 = jnp.sum(lp[0:1] * lp[1:2], axis=-1, keepdims=True)
    s2 = jnp.sum(lp[2:3] * lp[3:4], axis=-1, keepdims=True)
    return jnp.exp(s1) - jnp.exp(s2) + lam_init


def _flash_kernel(lam_ref, g_ref, q_ref, k_ref, v_ref, o_ref, kb_ref, vt_ref, m_ref, l_ref, acc_ref,
                  sa_ref, sb_ref, *, tq, lam_init):
    qi = pl.program_id(2)
    seq = k_ref.shape[1]

    @pl.when(qi == 0)
    def _():
        kb_ref[...] = k_ref[0].astype(BF16)
        for c in range(seq // LANES):
            sl = slice(c * LANES, (c + 1) * LANES)
            vt_ref[:, sl] = v_ref[0, sl, :].T.astype(BF16)

    qt = q_ref[0].T * (A_DK ** -0.5 * LOG2_E)
    chan = lax.broadcasted_iota(jnp.int32, qt.shape, 0)
    qt2 = jnp.concatenate([jnp.where(chan < A_DK, qt, 0.0), jnp.where(chan >= A_DK, qt, 0.0)],
                          axis=1).astype(BF16)
    m_ref[...] = jnp.full(m_ref.shape, -jnp.inf, F32)
    l_ref[...] = jnp.zeros(l_ref.shape, F32)
    acc_ref[...] = jnp.zeros(acc_ref.shape, F32)

    def scores(kb):
        start = pl.multiple_of(kb * tq, tq)
        return _dot(kb_ref[pl.ds(start, tq), :], qt2)

    def update(kb, s):
        start = pl.multiple_of(kb * tq, tq)
        m_prev = m_ref[...]
        m_new = jnp.maximum(m_prev, jnp.max(s, axis=0, keepdims=True))
        alpha = jnp.exp2(m_prev - m_new)
        p = jnp.exp2(s - m_new)
        l_ref[...] = alpha * l_ref[...] + jnp.sum(p, axis=0, keepdims=True)
        acc_ref[...] = alpha * acc_ref[...] + _dot(vt_ref[:, pl.ds(start, tq)], p.astype(BF16))
        m_ref[...] = m_new

    def diagonal(s_buf):
        s = s_buf[...]
        key = lax.broadcasted_iota(jnp.int32, s.shape, 0)
        qry = lax.broadcasted_iota(jnp.int32, s.shape, 1)
        qry = jnp.where(qry >= tq, qry - tq, qry)
        update(qi, jnp.where(key <= qry, s, -jnp.inf))

    sa_ref[...] = scores(0)

    def pair(i, carry):
        kb = 2 * i
        sb_ref[...] = scores(kb + 1)
        update(kb, sa_ref[...])
        sa_ref[...] = scores(kb + 2)
        update(kb + 1, sb_ref[...])
        return carry

    lax.fori_loop(0, qi // 2, pair, 0)

    @pl.when(qi % 2 == 0)
    def _():
        diagonal(sa_ref)

    @pl.when(qi % 2 == 1)
    def _():
        sb_ref[...] = scores(qi)
        update(qi - 1, sa_ref[...])
        diagonal(sb_ref)

    lam = _lam(lam_ref, lam_init)
    o = acc_ref[:, 0:tq] / l_ref[:, 0:tq] - lam * (acc_ref[:, tq:2 * tq] / l_ref[:, tq:2 * tq])
    o = o * lax.rsqrt(jnp.mean(o * o, axis=0, keepdims=True) + SUBLN_EPS)
    o_ref[0] = (o * (g_ref[...] * (1.0 - lam_init))).T


def flash_diff_attention(qkvz, lam_p, subln, lam_init, n_heads, tq_cap=512):
    b, s, _ = qkvz.shape
    tq = _row_tile(s, tq_cap)
    assert s % LANES == 0 and tq % LANES == 0
    kern = functools.partial(_flash_kernel, tq=tq, lam_init=lam_init)
    return pl.pallas_call(
        kern,
        grid=(b, n_heads, s // tq),
        in_specs=[pl.BlockSpec((4, A_DK), lambda bi, h, qi: (0, 0)),
                  pl.BlockSpec((A_DV, 1), lambda bi, h, qi: (0, 0)),
                  pl.BlockSpec((1, tq, A_DV), lambda bi, h, qi: (bi, qi, h)),
                  pl.BlockSpec((1, s, A_DV), lambda bi, h, qi: (bi, 0, n_heads + h)),
                  pl.BlockSpec((1, s, A_DV), lambda bi, h, qi: (bi, 0, 2 * n_heads + h))],
        out_specs=pl.BlockSpec((1, tq, A_DV), lambda bi, h, qi: (bi, qi, h)),
        out_shape=jax.ShapeDtypeStruct((b, s, n_heads * A_DV), F32),
        scratch_shapes=[pltpu.VMEM((s, A_DV), BF16), pltpu.VMEM((A_DV, s), BF16),
                        pltpu.VMEM((1, 2 * tq), F32), pltpu.VMEM((1, 2 * tq), F32),
                        pltpu.VMEM((A_DV, 2 * tq), F32), pltpu.VMEM((tq, 2 * tq), F32),
                        pltpu.VMEM((tq, 2 * tq), F32)],
        compiler_params=_cparams("parallel", "parallel", "arbitrary"),
        name="flash_diff_attention",
    )(lam_p, subln.reshape(A_DV, 1), qkvz, qkvz, qkvz)


def _paged_kernel(pt_ref, lam_ref, g_ref, x_ref, *rest, pages, n_steps, width, lam_init):
    kt_refs = rest[:pages]
    v_refs = rest[pages:2 * pages]
    o_ref, qm_ref, m_ref, l_ref, acc_ref, fin_ref = rest[2 * pages:]
    step = pl.program_id(1)
    heads = width // A_DV
    rows = 2 * heads
    page = kt_refs[0].shape[1]
    head_of_row = lax.broadcasted_iota(jnp.int32, (rows, A_DV), 0) // 2

    @pl.when(step == 0)
    def _():
        q = x_ref[0, :, 0:width] * (A_DK ** -0.5)
        r = lax.broadcasted_iota(jnp.int32, (rows, width), 0)
        lane = lax.broadcasted_iota(jnp.int32, (rows, width), 1)
        qm_ref[...] = jnp.where(lane // A_DK == r, q, 0.0)
        m_ref[...] = jnp.full(m_ref.shape, -jnp.inf, F32)
        l_ref[...] = jnp.zeros(l_ref.shape, F32)
        acc_ref[...] = jnp.zeros(acc_ref.shape, F32)

    qm = qm_ref[...].astype(BF16)
    s = jnp.concatenate([_dot(qm, kt_refs[i][...].astype(BF16)) for i in range(pages)], axis=-1)
    m_prev = m_ref[...]
    m_new = jnp.maximum(m_prev, jnp.max(s, axis=-1, keepdims=True))
    alpha = jnp.exp(m_prev - m_new)
    p = jnp.exp(s - m_new)
    l_ref[...] = alpha * l_ref[...] + jnp.sum(p, axis=-1, keepdims=True)
    p = p.astype(BF16)
    pv = jnp.zeros((rows, A_DV), F32)
    for i in range(pages):
        pi = p[:, i * page:(i + 1) * page]
        for h in range(heads):
            vh = v_refs[i][pl.ds(h, page, stride=heads), :].astype(BF16)
            pv = pv + jnp.where(head_of_row == h, _dot(pi, vh), 0.0)
    acc_ref[...] = alpha * acc_ref[...] + pv
    m_ref[...] = m_new

    @pl.when(step == n_steps - 1)
    def _():
        k_cur = x_ref[0, :, width:2 * width]
        s_cur = jnp.sum(qm_ref[...] * k_cur, axis=-1, keepdims=True)
        v_cur = jnp.zeros((rows, A_DV), F32)
        for h in range(heads):
            vh = x_ref[0, :, 2 * width + h * A_DV:2 * width + (h + 1) * A_DV]
            v_cur = jnp.where(head_of_row == h, vh, v_cur)
        m_fin = jnp.maximum(m_ref[...], s_cur)
        a_fin = jnp.exp(m_ref[...] - m_fin)
        p_cur = jnp.exp(s_cur - m_fin)
        l_fin = a_fin * l_ref[...] + p_cur
        fin_ref[...] = (a_fin * acc_ref[...] + p_cur * v_cur) / l_fin
        o = fin_ref[pl.ds(0, heads, stride=2), :] - _lam(lam_ref, lam_init) * fin_ref[pl.ds(1, heads, stride=2), :]
        o_ref[0] = _rms(o, SUBLN_EPS) * g_ref[...] * (1.0 - lam_init)


def paged_diff_attention(qkvz, cache_k, cache_v, page_table, layer, lam_p, subln, lam_init, pages=8):
    bsz, _, four_w = qkvz.shape
    width = four_w // 4
    heads = width // A_DV
    n_pages = page_table.shape[1]
    pages = math.gcd(pages, n_pages)
    n_steps = n_pages // pages
    n_layers, n_phys, page_size = cache_k.shape[:3]
    ckt = jnp.transpose(cache_k, (0, 1, 3, 4, 5, 2)).reshape(n_layers, n_phys, width, page_size)
    cv = cache_v.reshape(n_layers, n_phys, page_size * heads, A_DV)
    rows = 2 * heads

    def page_spec(i, shape):
        return pl.BlockSpec((None, None) + shape, lambda b, p, pt: (layer, pt[b, p * pages + i], 0, 0))

    kern = functools.partial(_paged_kernel, pages=pages, n_steps=n_steps, width=width, lam_init=lam_init)
    grid_spec = pltpu.PrefetchScalarGridSpec(
        num_scalar_prefetch=1,
        grid=(bsz, n_steps),
        in_specs=[pl.BlockSpec((4, A_DK), lambda b, p, pt: (0, 0)),
                  pl.BlockSpec((1, A_DV), lambda b, p, pt: (0, 0)),
                  pl.BlockSpec((1, 1, four_w), lambda b, p, pt: (b, 0, 0))]
                 + [page_spec(i, (width, page_size)) for i in range(pages)]
                 + [page_spec(i, (page_size * heads, A_DV)) for i in range(pages)],
        out_specs=pl.BlockSpec((1, heads, A_DV), lambda b, p, pt: (b, 0, 0)),
        scratch_shapes=[pltpu.VMEM((rows, width), F32), pltpu.VMEM((rows, 1), F32),
                        pltpu.VMEM((rows, 1), F32), pltpu.VMEM((rows, A_DV), F32),
                        pltpu.VMEM((rows, A_DV), F32)])
    return pl.pallas_call(
        kern,
        grid_spec=grid_spec,
        out_shape=jax.ShapeDtypeStruct((bsz, heads, A_DV), F32),
        compiler_params=_cparams("parallel", "arbitrary"),
        name="paged_diff_attention",
    )(page_table, lam_p, subln.reshape(1, A_DV), qkvz, *([ckt] * pages), *([cv] * pages))


def _rwkv_pre_kernel(*refs, has_vmix):
    (hn_ref, prev_ref, mix_ref, wrkvz_ref, w0_ref, w1_ref, w2_ref, a0_ref, a1_ref, a2_ref) = refs[:10]
    refs = refs[10:]
    if has_vmix:
        v0_ref, v1_ref, v2_ref, vf_ref = refs[:4]
        refs = refs[4:]
    r_ref, w_ref, k_ref, v_ref, a_ref, z_ref = refs

    hn = hn_ref[...]
    delta = prev_ref[...] - hn

    def mixed(m):
        return (hn + delta * mix_ref[m:m + 1, :]).astype(BF16)

    xv = mixed(2)
    r_ref[...] = _dot(mixed(0), wrkvz_ref[0])
    k_ref[...] = _dot(mixed(1), wrkvz_ref[1])
    v = _dot(xv, wrkvz_ref[2])
    z_ref[...] = _dot(mixed(3), wrkvz_ref[3])
    lw = _dot(jnp.tanh(_dot(mixed(4), w1_ref[...])).astype(BF16), w2_ref[...])
    x = w0_ref[...] + lw
    w_ref[...] = -(jnp.maximum(-x, 0.0) + jnp.log1p(jnp.exp(-jnp.abs(x)))) - 0.5
    la = _dot(_dot(mixed(5), a1_ref[...]).astype(BF16), a2_ref[...])
    a_ref[...] = _sigmoid(a0_ref[...] + la)
    if has_vmix:
        lv = _dot(_dot(xv, v1_ref[...]).astype(BF16), v2_ref[...])
        v = v + (vf_ref[...] - v) * _sigmoid(v0_ref[...] + lv)
    v_ref[...] = v


def rwkv_projections(hn, prev, mix, wrkvz, w0, w1, w2, a0, a1, a2, vmix=None):
    t, d = hn.shape
    tm = _row_tile(t, 256)
    row = pl.BlockSpec((tm, d), lambda i: (i, 0))

    def whole(x):
        return pl.BlockSpec(x.shape, lambda i: (0,) * x.ndim)

    args = [hn, prev, mix, wrkvz, w0.reshape(1, d), w1, w2, a0.reshape(1, d), a1, a2]
    in_specs = [row, row] + [whole(x) for x in args[2:]]
    if vmix is not None:
        v0, v1, v2, v_first = vmix
        extra = [v0.reshape(1, d), v1, v2]
        args += extra + [v_first]
        in_specs += [whole(x) for x in extra] + [row]
    return pl.pallas_call(
        functools.partial(_rwkv_pre_kernel, has_vmix=vmix is not None),
        grid=(t // tm,),
        in_specs=in_specs,
        out_specs=[row] * 6,
        out_shape=[jax.ShapeDtypeStruct((t, d), F32)] * 6,
        compiler_params=_cparams("parallel"),
        name="rwkv_projections",
    )(*args)


def _scan_kernel(r_ref, w_ref, k_ref, a_ref, v_ref, s0_ref, kkb_ref, ka_ref, rk_ref, lng_ref, lnb_ref,
                 o_ref, sfin_ref, st_ref, vec_ref, *, tb, n_tb, pitch):
    tblk = pl.program_id(1)

    @pl.when(tblk == 0)
    def _():
        st_ref[...] = s0_ref[0]

    def step(t, carry):
        key_rows = pl.ds(t, B_HEAD, stride=pitch)
        val_rows = pl.ds(t, SCAN_ROWS, stride=pitch)
        r = r_ref[key_rows, :]
        k = k_ref[key_rows, :]
        a = a_ref[key_rows, :]
        decay = jnp.exp(-jnp.exp(w_ref[key_rows, :]))
        kk = k * kkb_ref[...]
        kk = kk / jnp.maximum(jnp.sqrt(jnp.sum(kk * kk, axis=0, keepdims=True)), 1e-12)
        k = k * (1.0 + (a - 1.0) * ka_ref[...])
        neg_kk = -kk
        kka = kk * a
        dr = decay * r
        kka_r = jnp.sum(kka * r, axis=0, keepdims=True)
        k_r = jnp.sum(k * r, axis=0, keepdims=True)
        v = v_ref[val_rows, :]
        for n, tile in enumerate((neg_kk, dr, decay, kka, k)):
            vec_ref[n] = tile

        def row(n, j):
            return vec_ref[n, pl.ds(j, 1), :]

        sa = [jnp.zeros((SCAN_ROWS, LANES), F32) for _ in range(2)]
        so = [jnp.zeros((SCAN_ROWS, LANES), F32) for _ in range(2)]
        for j in range(B_HEAD):
            s = st_ref[j]
            sa[j % 2] = sa[j % 2] + s * row(0, j)
            so[j % 2] = so[j % 2] + s * row(1, j)
        sa = sa[0] + sa[1]
        so = so[0] + so[1]
        for j in range(B_HEAD):
            st_ref[j] = st_ref[j] * row(2, j) + sa * row(3, j) + v * row(4, j)
        o = so + sa * kka_r + v * k_r

        def head_mean(x):
            tot = jnp.sum(x, axis=0, keepdims=True)
            return (tot + pltpu.roll(tot, SCAN_HEADS, axis=1)) * (1.0 / B_HEAD)

        dev = o - head_mean(o)
        o = dev * lax.rsqrt(head_mean(dev * dev) + LNX_EPS) * lng_ref[...] + lnb_ref[...]
        bonus = jnp.sum(r * k * rk_ref[...], axis=0, keepdims=True)
        o_ref[val_rows, :] = o + bonus * v
        return carry

    lax.fori_loop(0, tb, step, 0, unroll=4 if tb % 4 == 0 else 1)

    @pl.when(tblk == n_tb - 1)
    def _():
        sfin_ref[0] = st_ref[...]


def _to_scan_kernel(x_ref, o_ref, ct_ref, *, rows, second, tb, pitch):
    bl, steps, d = x_ref.shape
    heads = d // B_HEAD
    pair = LANES // B_HEAD
    for b in range(bl):
        for c in range(d // LANES):
            xt = x_ref[b, :, c * LANES:(c + 1) * LANES].T
            for h in range(pair):
                ct_ref[pl.ds((b * heads + c * pair + h) * CT_PITCH, B_HEAD), :] = xt[h * B_HEAD:(h + 1) * B_HEAD]
    for r in range(rows):
        lo = ct_ref[pl.ds(r, SCAN_HEADS, stride=CT_PITCH), :]
        hi = ct_ref[pl.ds(second + r, SCAN_HEADS, stride=CT_PITCH), :]
        xt = jnp.concatenate([lo, hi], axis=0).T
        for q in range(steps // tb):
            o_ref[pl.ds((q * rows + r) * pitch, tb), :] = xt[q * tb:(q + 1) * tb]


def _from_scan_kernel(x_ref, o_ref, ct_ref, *, tb, pitch):
    bl, steps, d = o_ref.shape
    heads = d // B_HEAD
    for r in range(SCAN_ROWS):
        x = jnp.concatenate([x_ref[pl.ds((q * SCAN_ROWS + r) * pitch, tb), :] for q in range(steps // tb)], axis=0)
        xt = x.T
        for half in range(2):
            ct_ref[pl.ds((half * SCAN_ROWS + r) * CT_PITCH, SCAN_HEADS), :] = xt[half * SCAN_HEADS:(half + 1) * SCAN_HEADS]
    for b in range(bl):
        for c in range(d // LANES):
            first = b * heads + c * (LANES // B_HEAD)
            tile = jnp.concatenate([ct_ref[pl.ds(first + h, B_HEAD, stride=CT_PITCH), :]
                                    for h in range(LANES // B_HEAD)], axis=0)
            o_ref[b, :, c * LANES:(c + 1) * LANES] = tile.T


def to_scan_layout(x, bsz, seq, rows, second, tb):
    d = x.shape[1]
    pitch = _scan_pitch(tb)
    blk_rows = (LANES // tb) * rows * pitch
    out = pl.pallas_call(
        functools.partial(_to_scan_kernel, rows=rows, second=second, tb=tb, pitch=pitch),
        grid=(seq // LANES,),
        in_specs=[pl.BlockSpec((bsz, LANES, d), lambda t: (0, t, 0))],
        out_specs=pl.BlockSpec((blk_rows, LANES), lambda t: (t, 0)),
        out_shape=jax.ShapeDtypeStruct((seq // LANES * blk_rows, LANES), F32),
        scratch_shapes=[pltpu.VMEM((SCAN_HEADS * CT_PITCH, LANES), F32)],
        compiler_params=_cparams("parallel"),
        name="to_scan_layout",
    )(x.reshape(bsz, seq, d))
    return out.reshape(1, seq // tb, rows * pitch, LANES)


def from_scan_layout(o, bsz, seq, d, tb):
    pitch = _scan_pitch(tb)
    blk_rows = (LANES // tb) * SCAN_ROWS * pitch
    out = pl.pallas_call(
        functools.partial(_from_scan_kernel, tb=tb, pitch=pitch),
        grid=(seq // LANES,),
        in_specs=[pl.BlockSpec((blk_rows, LANES), lambda t: (t, 0))],
        out_specs=pl.BlockSpec((bsz, LANES, d), lambda t: (0, t, 0)),
        out_shape=jax.ShapeDtypeStruct((bsz, seq, d), F32),
        scratch_shapes=[pltpu.VMEM((B_HEAD * CT_PITCH, LANES), F32)],
        compiler_params=_cparams("parallel"),
        name="from_scan_layout",
    )(o.reshape(seq // LANES * blk_rows, LANES))
    return out.reshape(bsz * seq, d)


def _scan_pitch(tb):
    return tb if tb == 1 else tb + 8


def _scan_lane_param(p, heads):
    return jnp.tile(p.reshape(heads, B_HEAD).T, (1, LANES // heads))


def _scan_row_param(p, heads):
    x = p.reshape(heads, 2, SCAN_ROWS).transpose(2, 1, 0)
    x = jnp.broadcast_to(x[:, :, None, :], (SCAN_ROWS, 2, SCAN_HEADS // heads, heads))
    return x.reshape(SCAN_ROWS, LANES)


def wkv_scan(r, w, k, v, a, s0, kk_b, ka_b, rk_b, lnx_g, lnx_b, bsz, seq, tb_cap=32):
    d = r.shape[1]
    heads = d // B_HEAD
    bl = SCAN_HEADS // heads
    assert SCAN_HEADS % heads == 0 and bsz % bl == 0
    nblk = bsz // bl
    tb = _row_tile(seq, tb_cap)
    n_tb = seq // tb
    pitch = _scan_pitch(tb)

    def blocked(x, rows):
        x = x.reshape(nblk, n_tb, tb, rows, LANES).transpose(0, 1, 3, 2, 4)
        x = jnp.pad(x, ((0, 0), (0, 0), (0, 0), (0, pitch - tb), (0, 0)))
        return x.reshape(nblk, n_tb, rows * pitch, LANES)

    def key_layout(x):
        x = x.reshape(nblk, bl, seq, heads, B_HEAD).transpose(0, 2, 4, 1, 3)
        x = x.reshape(nblk, seq, B_HEAD, SCAN_HEADS)
        return blocked(jnp.concatenate([x, x], axis=-1), B_HEAD)

    def val_layout(x):
        x = x.reshape(nblk, bl, seq, heads, 2, SCAN_ROWS).transpose(0, 2, 5, 4, 1, 3)
        return blocked(x.reshape(nblk, seq, SCAN_ROWS, LANES), SCAN_ROWS)

    in_kernel_layout = nblk == 1 and seq % LANES == 0
    if in_kernel_layout:
        key_layout = functools.partial(to_scan_layout, bsz=bsz, seq=seq, rows=B_HEAD, second=0, tb=tb)
        val_layout = functools.partial(to_scan_layout, bsz=bsz, seq=seq, rows=SCAN_ROWS, second=SCAN_ROWS, tb=tb)

    st0 = s0.astype(F32).reshape(nblk, bl, heads, 2, SCAN_ROWS, B_HEAD).transpose(0, 5, 4, 3, 1, 2)
    st0 = st0.reshape(nblk, B_HEAD, SCAN_ROWS, LANES)

    key_spec = pl.BlockSpec((None, None, B_HEAD * pitch, LANES), lambda n, t: (n, t, 0, 0))
    val_spec = pl.BlockSpec((None, None, SCAN_ROWS * pitch, LANES), lambda n, t: (n, t, 0, 0))
    st_spec = pl.BlockSpec((1, B_HEAD, SCAN_ROWS, LANES), lambda n, t: (n, 0, 0, 0))
    kp_spec = pl.BlockSpec((B_HEAD, LANES), lambda n, t: (0, 0))
    vp_spec = pl.BlockSpec((SCAN_ROWS, LANES), lambda n, t: (0, 0))
    o, sfin = pl.pallas_call(
        functools.partial(_scan_kernel, tb=tb, n_tb=n_tb, pitch=pitch),
        grid=(nblk, n_tb),
        in_specs=[key_spec] * 4 + [val_spec, st_spec] + [kp_spec] * 3 + [vp_spec] * 2,
        out_specs=[val_spec, st_spec],
        out_shape=[jax.ShapeDtypeStruct((nblk, n_tb, SCAN_ROWS * pitch, LANES), F32),
                   jax.ShapeDtypeStruct((nblk, B_HEAD, SCAN_ROWS, LANES), F32)],
        scratch_shapes=[pltpu.VMEM((B_HEAD, SCAN_ROWS, LANES), F32), pltpu.VMEM((5, B_HEAD, LANES), F32)],
        compiler_params=_cparams("parallel", "arbitrary"),
        name="wkv_scan",
    )(key_layout(r), key_layout(w), key_layout(k), key_layout(a), val_layout(v), st0,
      _scan_lane_param(kk_b, heads), _scan_lane_param(ka_b, heads), _scan_lane_param(rk_b.reshape(-1), heads),
      _scan_row_param(lnx_g, heads), _scan_row_param(lnx_b, heads))
    if in_kernel_layout:
        o = from_scan_layout(o, bsz, seq, d, tb)
    else:
        o = o.reshape(nblk, n_tb, SCAN_ROWS, pitch, 2, bl, heads)[:, :, :, :tb]
        o = o.transpose(0, 5, 1, 3, 6, 4, 2).reshape(bsz * seq, d)
    sfin = sfin.reshape(nblk, B_HEAD, SCAN_ROWS, 2, bl, heads).transpose(0, 4, 5, 3, 2, 1)
    return o, sfin.reshape(bsz, heads, B_HEAD, B_HEAD)


def _post_kernel(*refs, final):
    h_ref, o_ref, z_ref, wout_ref, p_ref, wpe_ref, wpg_ref = refs[:7]
    refs = refs[7:]
    if final:
        nf_ref, out_ref = refs
    else:
        (out_ref,) = refs
    z = z_ref[...]
    x = (o_ref[...] * (z * _sigmoid(z))).astype(BF16)
    h = h_ref[...] + _dot(x, wout_ref[...])
    gate = _sigmoid(_dot(_rms(h, NORM_EPS).astype(BF16), wpg_ref[...]))
    h = h + _dot(p_ref[...].astype(BF16), wpe_ref[...]) * gate
    if final:
        h = _rms(h, NORM_EPS) * nf_ref[...]
    out_ref[...] = h


def mixer_output(h, o, z_src, z_col, w_out, p, w_pe, w_pg, norm_f=None):
    t, d = h.shape
    tm = _row_tile(t, 512)
    row = pl.BlockSpec((tm, d), lambda i: (i, 0))

    def whole(x):
        return pl.BlockSpec(x.shape, lambda i: (0,) * x.ndim)

    args = [h, o, z_src, w_out, p, w_pe, w_pg]
    in_specs = [row, row, pl.BlockSpec((tm, d), lambda i: (i, z_col)), whole(w_out),
                pl.BlockSpec((tm, p.shape[1]), lambda i: (i, 0)), whole(w_pe), whole(w_pg)]
    if norm_f is not None:
        args.append(norm_f.reshape(1, d))
        in_specs.append(whole(args[-1]))
    return pl.pallas_call(
        functools.partial(_post_kernel, final=norm_f is not None),
        grid=(t // tm,),
        in_specs=in_specs,
        out_specs=row,
        out_shape=jax.ShapeDtypeStruct((t, d), F32),
        compiler_params=_cparams("parallel"),
        name="mixer_output",
    )(*args)


def _run_trunk(x, p, wts, paged=None, wkv_init=None, shift_init=None):
    bsz, seq, d = x.shape
    t = bsz * seq
    depth = p.shape[0]
    n_a_heads = d // A_DV
    h = x.reshape(t, d)
    v_first = None
    new_k, new_v, new_wkv, new_shift = [], [], [], []
    for i in range(depth):
        j = i // N_MIXERS
        last = i == depth - 1
        if i % N_MIXERS == 0:
            lam_init = 0.8 - 0.6 * math.exp(-0.3 * i)
            qkvz = norm_matmul(h, wts['norm_g'][i], wts['w_in_a'][j])
            new_k.append(qkvz[:, d:2 * d].reshape(bsz, seq, n_a_heads, 2, A_DK))
            new_v.append(qkvz[:, 2 * d:3 * d].reshape(bsz, seq, n_a_heads, A_DV))
            qkvz3 = qkvz.reshape(bsz, seq, 4 * d)
            if paged is None:
                o = flash_diff_attention(qkvz3, wts['lam_a'][j], wts['subln_a'][j], lam_init, n_a_heads)
            else:
                ck, cv, pt = paged
                o = paged_diff_attention(qkvz3, ck, cv, pt, j, wts['lam_a'][j], wts['subln_a'][j], lam_init)
            o = o.reshape(t, d)
            z_src, z_col, w_out = qkvz, 3, wts['w_out_a'][j]
        else:
            hn = rms_norm_call(h, wts['norm_g'][i])
            hn3 = hn.reshape(bsz, seq, d)
            if wkv_init is None:
                s0 = jnp.zeros((bsz, d // B_HEAD, B_HEAD, B_HEAD), F32)
                sh0 = jnp.zeros((bsz, d), F32)
            else:
                s0, sh0 = wkv_init[j], shift_init[j]
            prev = jnp.concatenate([sh0[:, None, :], hn3[:, :-1]], axis=1).reshape(t, d)
            vmix = None
            if v_first is not None:
                vmix = (wts['v0_b'][j - 1], wts['v1_b'][j - 1], wts['v2_b'][j - 1], v_first)
            r, w, k, v, a, z = rwkv_projections(
                hn, prev, wts['mix_b'][j], wts['w_rkvz_b'][j], wts['w0_b'][j], wts['w1_b'][j], wts['w2_b'][j],
                wts['a0_b'][j], wts['a1_b'][j], wts['a2_b'][j], vmix)
            if v_first is None:
                v_first = v
            o, s_fin = wkv_scan(r, w, k, v, a, s0, wts['kk_b'][j], wts['ka_b'][j], wts['rk_b'][j],
                                wts['lnx_g_b'][j], wts['lnx_b_b'][j], bsz, seq)
            new_wkv.append(s_fin)
            new_shift.append(hn3[:, -1])
            z_src, z_col, w_out = z, 0, wts['w_out_b'][j]
        h = mixer_output(h, o, z_src, z_col, w_out, p[i].reshape(t, -1), wts['w_pe'][i], wts['w_pg'][i],
                         wts['norm_f'] if last else None)
    return (h.reshape(bsz, seq, d), jnp.stack(new_k), jnp.stack(new_v), jnp.stack(new_wkv),
            jnp.stack(new_shift))


_MATMUL_WEIGHTS = ('w_pe', 'w_pg', 'w_in_a', 'w_out_a', 'w_rkvz_b', 'w1_b', 'w2_b', 'a1_b', 'a2_b',
                   'v1_b', 'v2_b', 'w_out_b')


def kernel(x_prompt, x_sample, cache_k, cache_v, state_wkv, state_shift, page_table, p_prompt, p_sample, norm_g, norm_f, w_pe, w_pg, w_in_a, w_out_a, lam_a, subln_a, mix_b, w_rkvz_b, w0_b, w1_b, w2_b, a0_b, a1_b, a2_b, v0_b, v1_b, v2_b, kk_b, ka_b, rk_b, lnx_g_b, lnx_b_b, w_out_b):
    wts = dict(norm_g=norm_g, norm_f=norm_f, w_pe=w_pe, w_pg=w_pg, w_in_a=w_in_a,
               w_out_a=w_out_a, lam_a=lam_a, subln_a=subln_a, mix_b=mix_b, w_rkvz_b=w_rkvz_b,
               w0_b=w0_b, w1_b=w1_b, w2_b=w2_b, a0_b=a0_b, a1_b=a1_b, a2_b=a2_b,
               v0_b=v0_b, v1_b=v1_b, v2_b=v2_b, kk_b=kk_b, ka_b=ka_b, rk_b=rk_b,
               lnx_g_b=lnx_g_b, lnx_b_b=lnx_b_b, w_out_b=w_out_b)
    for name in _MATMUL_WEIGHTS:
        wts[name] = wts[name].astype(BF16)
    y_p, k_p, v_p, wkv_p, sh_p = _run_trunk(x_prompt, p_prompt, wts)
    y_s, k_s, v_s, wkv_s, sh_s = _run_trunk(x_sample, p_sample, wts, paged=(cache_k, cache_v, page_table),
                                            wkv_init=state_wkv, shift_init=state_shift)
    return (y_p, y_s, k_p, v_p, wkv_p, sh_p, k_s, v_s, wkv_s, sh_s)
```
